```python
import jax, jax.numpy as jnp
from jax import lax
import numpy as np

D_MODEL = 2048
BATCH = 2
SEQ = 4096
DEPTH = 1

HEAD_DIM = 128
ATTN_GROUPS = ((128, 1), (512, 4), (2048, 16))
N_GROUPS = len(ATTN_GROUPS)
HEADS_PER_GROUP = 4
N_ATTN_HEADS = N_GROUPS * HEADS_PER_GROUP
ATTN_WIDTH = N_ATTN_HEADS * HEAD_DIM
ATTN_OUT_WIDTH = HEADS_PER_GROUP * HEAD_DIM
ROPE_THETA = 10000.0
Q_BLOCK = 128
CONV_WIDTH = D_MODEL // 2
CONV_K = 3
IN_COLS = 3 * CONV_WIDTH + 3 * ATTN_WIDTH + 2 * D_MODEL
N_EXPERTS = 64
TOP_K = 8
N_EXPERT_GROUPS = 8
TOP_GROUPS = 4
D_EXPERT = D_MODEL // 4
ROUTED_SCALE = 2.5
EPS = 1e-6
NEG_INF = -1e30

kernel_name = "hybrid_conv_dilated_attn_moe_block"


def rmsnorm(x, g):
    xf = x.astype(jnp.float32)
    xf = xf * lax.rsqrt(jnp.mean(xf * xf, axis=-1, keepdims=True) + EPS)
    return (xf * g.astype(jnp.float32)).astype(x.dtype)


def rope_tables(positions):
    inv = ROPE_THETA ** (-jnp.arange(0, HEAD_DIM, 2, dtype=jnp.float32) / HEAD_DIM)
    ang = positions.astype(jnp.float32)[..., None] * inv
    return jnp.cos(ang)[:, :, None, :], jnp.sin(ang)[:, :, None, :]


def apply_rope(t, cos, sin):
    tf = t.astype(jnp.float32)
    t1, t2 = jnp.split(tf, 2, axis=-1)
    return jnp.concatenate([t1 * cos - t2 * sin, t2 * cos + t1 * sin], axis=-1).astype(t.dtype)


def dilated_window_attention(q, k, v, window, dilation):
    B, S, H, Dh = q.shape
    r = dilation
    band = window // r
    n_sub = S // r
    nb = -(-n_sub // Q_BLOCK)
    n_pad = nb * Q_BLOCK

    def to_sub(t):
        return t.reshape(B, n_sub, r, H, Dh).transpose(0, 3, 2, 1, 4)

    qs, ks, vs = to_sub(q), to_sub(k), to_sub(v)
    tail = n_pad - n_sub
    qs = jnp.pad(qs, ((0, 0), (0, 0), (0, 0), (0, tail), (0, 0)))
    kv_pad = ((0, 0), (0, 0), (0, 0), (Q_BLOCK, tail), (0, 0))
    ks_p, vs_p = jnp.pad(ks, kv_pad), jnp.pad(vs, kv_pad)

    def blocks(t):
        prev = t[..., :n_pad, :].reshape(B, H, r, nb, Q_BLOCK, Dh)
        cur = t[..., Q_BLOCK:, :].reshape(B, H, r, nb, Q_BLOCK, Dh)
        return jnp.concatenate([prev, cur], axis=-2)

    kb, vb = blocks(ks_p), blocks(vs_p)
    qb = qs.reshape(B, H, r, nb, Q_BLOCK, Dh)
    scores = jnp.einsum('bhrnqd,bhrnkd->bhrnqk', qb, kb,
                        preferred_element_type=jnp.float32)
    qi = jnp.arange(Q_BLOCK)[:, None]
    kj = jnp.arange(2 * Q_BLOCK)[None, :]
    dist = qi - kj + Q_BLOCK
    key_sub = jnp.arange(nb)[:, None, None] * Q_BLOCK + kj[None] - Q_BLOCK
    mask = (dist >= 0)[None] & (dist <= band)[None] & (key_sub >= 0)
    scores = jnp.where(mask, scores, NEG_INF)
    m = jnp.max(scores, axis=-1, keepdims=True)
    p = jnp.exp(scores - m)
    s = jnp.sum(p, axis=-1, keepdims=True)
    o = jnp.einsum('bhrnqk,bhrnkd->bhrnqd', p.astype(vb.dtype), vb,
                   preferred_element_type=jnp.float32) / s
    lse = (m + jnp.log(s))[..., 0]
    o = o.reshape(B, H, r, n_pad, Dh)[..., :n_sub, :].transpose(0, 3, 2, 1, 4).reshape(B, S, H, Dh)
    lse = lse.reshape(B, H, r, n_pad)[..., :n_sub].transpose(0, 3, 2, 1).reshape(B, S, H)
    return o, lse


def dilated_mixture_attention(q, k, v, positions):
    B, S = q.shape[:2]
    cos, sin = rope_tables(positions)
    q = apply_rope(q, cos, sin) * (HEAD_DIM ** -0.5)
    k = apply_rope(k, cos, sin)
    q = q.reshape(B, S, N_GROUPS, HEADS_PER_GROUP, HEAD_DIM)
    k = k.reshape(B, S, N_GROUPS, HEADS_PER_GROUP, HEAD_DIM)
    v = v.reshape(B, S, N_GROUPS, HEADS_PER_GROUP, HEAD_DIM)
    outs, lses = [], []
    for g, (window, dilation) in enumerate(ATTN_GROUPS):
        o_g, l_g = dilated_window_attention(q[:, :, g], k[:, :, g], v[:, :, g], window, dilation)
        outs.append(o_g)
        lses.append(l_g)
    wts = jax.nn.softmax(jnp.stack(lses, axis=0), axis=0)
    o = jnp.einsum('gbsh,gbshd->bshd', wts, jnp.stack(outs, axis=0))
    return o.reshape(B, S, ATTN_OUT_WIDTH).astype(v.dtype)


def short_gated_conv(b_gate, c_gate, v_in, conv_w):
    S = v_in.shape[1]
    u = c_gate * v_in
    up = jnp.pad(u, ((0, 0), (CONV_K - 1, 0), (0, 0)))
    conv = sum(conv_w[tap] * up[:, tap:tap + S] for tap in range(CONV_K))
    return b_gate * conv


def swiglu(t, wg, wu, wd):
    return (jax.nn.silu(t @ wg) * (t @ wu)) @ wd


def moe_ffn(h, w_router, router_bias, w_exp_gate, w_exp_up, w_exp_down, w_sh_gate, w_sh_up, w_sh_down):
    B, S, D = h.shape
    T = B * S
    t = h.reshape(T, D)
    scores = jax.nn.sigmoid(jnp.matmul(t, w_router, preferred_element_type=jnp.float32))
    sel = scores + router_bias.astype(jnp.float32)
    group_score = lax.top_k(sel.reshape(T, N_EXPERT_GROUPS, N_EXPERTS // N_EXPERT_GROUPS), 2)[0].sum(-1)
    _, top_g = lax.top_k(group_score, TOP_GROUPS)
    gmask = jax.nn.one_hot(top_g, N_EXPERT_GROUPS, dtype=jnp.float32).sum(1) > 0
    emask = jnp.repeat(gmask, N_EXPERTS // N_EXPERT_GROUPS, axis=1)
    _, top_e = lax.top_k(jnp.where(emask, sel, NEG_INF), TOP_K)
    w = jnp.take_along_axis(scores, top_e, axis=1)
    w = w / jnp.sum(w, axis=-1, keepdims=True) * ROUTED_SCALE
    gates = jnp.einsum('tk,tke->te', w, jax.nn.one_hot(top_e, N_EXPERTS, dtype=jnp.float32))

    def expert_step(acc, ex):
        wg, wu, wd, g = ex
        return acc + g[:, None].astype(t.dtype) * swiglu(t, wg, wu, wd), None

    routed, _ = lax.scan(expert_step, jnp.zeros_like(t),
                         (w_exp_gate, w_exp_up, w_exp_down, gates.T))
    shared = swiglu(t, w_sh_gate, w_sh_up, w_sh_down)
    return (routed + shared).reshape(B, S, D)


def setup_inputs(seed: int = 0) -> dict:
    key = jax.random.key(seed)
    ks = jax.random.split(key, 24)
    f32 = jnp.float32

    def nrm(k, shape, scale):
        return jax.random.normal(k, shape, f32) * scale

    L, D = DEPTH, D_MODEL
    return {
        "x": nrm(ks[0], (BATCH, SEQ, D), 1.0),
        "c": nrm(ks[1], (BATCH, D), 1.0),
        "positions": jnp.broadcast_to(jnp.arange(SEQ, dtype=jnp.int32), (BATCH, SEQ)),
        "norm_mix_g": 1.0 + nrm(ks[2], (L, D), 0.1),
        "w_ada": nrm(ks[3], (L, D, 6 * D), 0.5 * D ** -0.5),
        "b_ada": nrm(ks[4], (L, 6 * D), 0.01),
        "w_in": nrm(ks[5], (L, D, IN_COLS), D ** -0.5),
        "conv_w": nrm(ks[6], (L, CONV_K, CONV_WIDTH), CONV_K ** -0.5),
        "w_conv_out": nrm(ks[7], (L, CONV_WIDTH, D), CONV_WIDTH ** -0.5),
        "w_attn_out": nrm(ks[8], (L, ATTN_OUT_WIDTH, D), ATTN_OUT_WIDTH ** -0.5),
        "w_o": nrm(ks[9], (L, D, D), D ** -0.5),
        "norm_ffn_g": 1.0 + nrm(ks[10], (L, D), 0.1),
        "w_router": nrm(ks[11], (L, D, N_EXPERTS), D ** -0.5),
        "router_bias": nrm(ks[12], (L, N_EXPERTS), 0.01),
        "w_exp_gate": nrm(ks[13], (L, N_EXPERTS, D, D_EXPERT), D ** -0.5),
        "w_exp_up": nrm(ks[14], (L, N_EXPERTS, D, D_EXPERT), D ** -0.5),
        "w_exp_down": nrm(ks[15], (L, N_EXPERTS, D_EXPERT, D), D_EXPERT ** -0.5),
        "w_sh_gate": nrm(ks[16], (L, D, D_EXPERT), D ** -0.5),
        "w_sh_up": nrm(ks[17], (L, D, D_EXPERT), D ** -0.5),
        "w_sh_down": nrm(ks[18], (L, D_EXPERT, D), D_EXPERT ** -0.5),
        "norm_final_g": 1.0 + nrm(ks[19], (D,), 0.1),
    }


def reference(x, c, positions, norm_mix_g, w_ada, b_ada, w_in, conv_w, w_conv_out, w_attn_out,
              w_o, norm_ffn_g, w_router, router_bias, w_exp_gate, w_exp_up, w_exp_down,
              w_sh_gate, w_sh_up, w_sh_down, norm_final_g):
    B, S, D = x.shape
    split_at = [CONV_WIDTH, 2 * CONV_WIDTH, 3 * CONV_WIDTH,
                3 * CONV_WIDTH + ATTN_WIDTH, 3 * CONV_WIDTH + 2 * ATTN_WIDTH,
                3 * CONV_WIDTH + 3 * ATTN_WIDTH, 3 * CONV_WIDTH + 3 * ATTN_WIDTH + D]
    for l in range(DEPTH):
        mod = jax.nn.silu(c) @ w_ada[l] + b_ada[l]
        shift1, scale1, gate1, shift2, scale2, gate2 = [m[:, None, :] for m in jnp.split(mod, 6, axis=-1)]

        h = rmsnorm(x, norm_mix_g[l]) * (1.0 + scale1) + shift1
        proj = h @ w_in[l]
        b_c, c_c, v_c, q, k, v, g_conv, g_attn = jnp.split(proj, split_at, axis=-1)
        y_conv = short_gated_conv(b_c, c_c, v_c, conv_w[l]) @ w_conv_out[l]
        y_attn = dilated_mixture_attention(q.reshape(B, S, N_ATTN_HEADS, HEAD_DIM),
                                           k.reshape(B, S, N_ATTN_HEADS, HEAD_DIM),
                                           v.reshape(B, S, N_ATTN_HEADS, HEAD_DIM),
                                           positions) @ w_attn_out[l]
        merged = jax.nn.sigmoid(g_conv) * y_conv + jax.nn.sigmoid(g_attn) * y_attn
        x = x + gate1 * (merged @ w_o[l])

        h2 = rmsnorm(x, norm_ffn_g[l]) * (1.0 + scale2) + shift2
        x = x + gate2 * moe_ffn(h2, w_router[l], router_bias[l], w_exp_gate[l], w_exp_up[l],
                                w_exp_down[l], w_sh_gate[l], w_sh_up[l], w_sh_down[l])
    return rmsnorm(x, norm_final_g)
```

```python
import functools

import jax
import jax.numpy as jnp
from jax import lax
from jax.experimental import pallas as pl
from jax.experimental.pallas import tpu as pltpu

D_MODEL = 2048
HEAD_DIM = 128
ATTN_DILATIONS = (1, 4, 16)
N_GROUPS = 3
HEADS_PER_GROUP = 4
ATTN_WIDTH = N_GROUPS * HEADS_PER_GROUP * HEAD_DIM
ATTN_OUT_WIDTH = HEADS_PER_GROUP * HEAD_DIM
ROPE_THETA = 10000.0
Q_BLOCK = 128
CONV_WIDTH = D_MODEL // 2
CONV_K = 3
IN_COLS = 3 * CONV_WIDTH + 3 * ATTN_WIDTH + 2 * D_MODEL
N_EXPERTS = 64
TOP_K = 8
N_EXPERT_GROUPS = 8
TOP_GROUPS = 4
D_EXPERT = D_MODEL // 4
ROUTED_SCALE = 2.5
EPS = 1e-6
NEG_INF = -1e30

F32 = jnp.float32
BF16 = jnp.bfloat16
MIB = 1024 * 1024

PROJ_BLOCK = 512
N_PROJ_BLOCKS = IN_COLS // PROJ_BLOCK
GATE_SRC_BLOCK = (3 * CONV_WIDTH + 3 * ATTN_WIDTH) // PROJ_BLOCK
N_GATE_BLOCKS = 2 * D_MODEL // PROJ_BLOCK
COL_GCONV = 0
COL_GATTN = D_MODEL
COL_B = 2 * D_MODEL
COL_C = COL_B + CONV_WIDTH
COL_V = COL_C + CONV_WIDTH
COL_Q = COL_V + CONV_WIDTH
COL_K = COL_Q + ATTN_WIDTH
COL_VA = COL_K + ATTN_WIDTH


def _params(semantics, vmem_mib):
    return pltpu.CompilerParams(dimension_semantics=semantics, vmem_limit_bytes=vmem_mib * MIB)


def _sigmoid(x):
    return 1.0 / (1.0 + jnp.exp(-x))


def _ada_kernel(c_ref, w_ref, b_ref, o_ref):
    c = c_ref[...]
    s = c * _sigmoid(c)
    o_ref[...] = jnp.dot(s, w_ref[...], precision=lax.Precision.HIGHEST,
                         preferred_element_type=F32) + b_ref[...]


def _ada(c, w_ada, b_ada):
    B, D = c.shape
    N = w_ada.shape[1]
    tn = 1024
    return pl.pallas_call(
        _ada_kernel,
        grid=(N // tn,),
        in_specs=[pl.BlockSpec((B, D), lambda j: (0, 0)),
                  pl.BlockSpec((D, tn), lambda j: (0, j)),
                  pl.BlockSpec((1, tn), lambda j: (0, j))],
        out_specs=pl.BlockSpec((B, tn), lambda j: (0, j)),
        out_shape=jax.ShapeDtypeStruct((B, N), F32),
        compiler_params=_params(("arbitrary",), 40),
        name="ada_mod",
    )(c, w_ada, b_ada.reshape(1, N))


def _modnorm(x, g, scale, shift):
    xf = x * lax.rsqrt(jnp.mean(x * x, axis=-1, keepdims=True) + EPS)
    return (xf * g) * (1.0 + scale) + shift


def _norm1_kernel(x_ref, g_ref, sc_ref, sh_ref, o_ref):
    o_ref[...] = _modnorm(x_ref[...], g_ref[...], sc_ref[...], sh_ref[...]).astype(o_ref.dtype)


def _norm1(x2d, g, mod3, seq):
    T, D = x2d.shape
    tm = 512
    per_seq = seq // tm
    return pl.pallas_call(
        _norm1_kernel,
        grid=(T // tm,),
        in_specs=[pl.BlockSpec((tm, D), lambda i: (i, 0)),
                  pl.BlockSpec((1, D), lambda i: (0, 0)),
                  pl.BlockSpec((None, 1, D), lambda i: ((i // per_seq) * 6 + 1, 0, 0)),
                  pl.BlockSpec((None, 1, D), lambda i: ((i // per_seq) * 6 + 0, 0, 0))],
        out_specs=pl.BlockSpec((tm, D), lambda i: (i, 0)),
        out_shape=jax.ShapeDtypeStruct((T, D), BF16),
        compiler_params=_params(("arbitrary",), 32),
        name="norm_mix",
    )(x2d, g, mod3, mod3)


def _inproj_kernel(h_ref, w_ref, o_ref, wbf_ref):
    @pl.when(pl.program_id(1) == 0)
    def _():
        wbf_ref[...] = w_ref[...].astype(BF16)

    o_ref[...] = jnp.dot(h_ref[...], wbf_ref[...], preferred_element_type=F32).astype(o_ref.dtype)


def _proj_dst_block(j):
    return jnp.where(j < GATE_SRC_BLOCK, j + N_GATE_BLOCKS, j - GATE_SRC_BLOCK)


def _inproj(h, w_in):
    T, D = h.shape
    tm = 2048
    tn = PROJ_BLOCK
    return pl.pallas_call(
        _inproj_kernel,
        grid=(N_PROJ_BLOCKS, T // tm),
        in_specs=[pl.BlockSpec((tm, D), lambda j, i: (i, 0)),
                  pl.BlockSpec((D, tn), lambda j, i: (0, j))],
        out_specs=pl.BlockSpec((tm, tn), lambda j, i: (i, _proj_dst_block(j))),
        out_shape=jax.ShapeDtypeStruct((T, IN_COLS), BF16),
        scratch_shapes=[pltpu.VMEM((D, tn), BF16)],
        compiler_params=_params(("arbitrary", "arbitrary"), 48),
        name="in_proj",
    )(h, w_in)


def _rope_kernel(pos_ref, inv_ref, sign_ref, cos_ref, sin_ref):
    ang = pos_ref[...].astype(F32) * inv_ref[...]
    cos_ref[...] = jnp.cos(ang)
    sin_ref[...] = jnp.sin(ang) * sign_ref[...]


def _rope_tables(positions):
    T = positions.size
    ts = 1024
    inv = ROPE_THETA ** (-jnp.arange(0, HEAD_DIM, 2, dtype=F32) / HEAD_DIM)
    inv2 = jnp.concatenate([inv, inv]).reshape(1, HEAD_DIM)
    sign = jnp.concatenate([-jnp.ones((HEAD_DIM // 2,), F32), jnp.ones((HEAD_DIM // 2,), F32)]).reshape(1, HEAD_DIM)
    return pl.pallas_call(
        _rope_kernel,
        grid=(T // ts,),
        in_specs=[pl.BlockSpec((ts, 1), lambda i: (i, 0)),
                  pl.BlockSpec((1, HEAD_DIM), lambda i: (0, 0)),
                  pl.BlockSpec((1, HEAD_DIM), lambda i: (0, 0))],
        out_specs=[pl.BlockSpec((ts, HEAD_DIM), lambda i: (i, 0)),
                   pl.BlockSpec((ts, HEAD_DIM), lambda i: (i, 0))],
        out_shape=[jax.ShapeDtypeStruct((T, HEAD_DIM), F32)] * 2,
        compiler_params=_params(("arbitrary",), 32),
        name="rope_tables",
    )(positions.reshape(T, 1), inv2, sign)


ROW_CHUNK = 512


def _attn_group(r, seq, q_ref, k_ref, v_ref, cos_ref, sin_ref, nat_ref, qd_ref, kd_ref, vd_ref,
                acc_ref, m_ref, s_ref):
    n_sub = seq // r
    nb = n_sub // Q_BLOCK
    cls = n_sub + Q_BLOCK
    n_chunks = seq // ROW_CHUNK
    scale = HEAD_DIM ** -0.5

    def rope(t_ref, c0, mult):
        t = t_ref[pl.ds(c0, ROW_CHUNK), :].astype(F32)
        cs = cos_ref[pl.ds(c0, ROW_CHUNK), :]
        sn = sin_ref[pl.ds(c0, ROW_CHUNK), :]
        out = t * cs + pltpu.roll(t, HEAD_DIM // 2, 1) * sn
        return out * mult if mult is not None else out

    def deinterleave(dst_ref, dst_stride, dst_off):
        for rho in range(r):
            dst_ref[pl.ds(rho * dst_stride + dst_off, n_sub), :] = (
                nat_ref[pl.ds(rho, n_sub, stride=r), :].astype(BF16))

    zeros_blk = jnp.zeros((Q_BLOCK, HEAD_DIM), BF16)
    for rho in range(r):
        kd_ref[pl.ds(rho * cls, Q_BLOCK), :] = zeros_blk
        vd_ref[pl.ds(rho * cls, Q_BLOCK), :] = zeros_blk

    if r == 1:
        def fill(c, _):
            c0 = pl.multiple_of(c * ROW_CHUNK, ROW_CHUNK)
            qd_ref[pl.ds(c0, ROW_CHUNK), :] = rope(q_ref, c0, scale).astype(BF16)
            kd_ref[pl.ds(Q_BLOCK + c0, ROW_CHUNK), :] = rope(k_ref, c0, None).astype(BF16)
            vd_ref[pl.ds(Q_BLOCK + c0, ROW_CHUNK), :] = v_ref[pl.ds(c0, ROW_CHUNK), :]
            return 0
        lax.fori_loop(0, n_chunks, fill, 0)
    else:
        def fill_q(c, _):
            c0 = pl.multiple_of(c * ROW_CHUNK, ROW_CHUNK)
            nat_ref[pl.ds(c0, ROW_CHUNK), :] = rope(q_ref, c0, scale)
            return 0
        lax.fori_loop(0, n_chunks, fill_q, 0)
        deinterleave(qd_ref, n_sub, 0)

        def fill_k(c, _):
            c0 = pl.multiple_of(c * ROW_CHUNK, ROW_CHUNK)
            nat_ref[pl.ds(c0, ROW_CHUNK), :] = rope(k_ref, c0, None)
            return 0
        lax.fori_loop(0, n_chunks, fill_k, 0)
        deinterleave(kd_ref, cls, Q_BLOCK)

        def fill_v(c, _):
            c0 = pl.multiple_of(c * ROW_CHUNK, ROW_CHUNK)
            nat_ref[pl.ds(c0, ROW_CHUNK), :] = v_ref[pl.ds(c0, ROW_CHUNK), :].astype(F32)
            return 0
        lax.fori_loop(0, n_chunks, fill_v, 0)
        deinterleave(vd_ref, cls, Q_BLOCK)

    qi = lax.broadcasted_iota(jnp.int32, (Q_BLOCK, 2 * Q_BLOCK), 0)
    kj = lax.broadcasted_iota(jnp.int32, (Q_BLOCK, 2 * Q_BLOCK), 1)
    cur_ok = (kj >= Q_BLOCK) & (kj - Q_BLOCK <= qi)
    prev_ok = (kj < Q_BLOCK) & (kj >= qi)

    def block(blk, _):
        rho = blk // nb
        n = blk % nb
        qrow = pl.multiple_of(rho * n_sub + n * Q_BLOCK, Q_BLOCK)
        krow = pl.multiple_of(rho * cls + n * Q_BLOCK, Q_BLOCK)
        q = qd_ref[pl.ds(qrow, Q_BLOCK), :]
        kw = kd_ref[pl.ds(krow, 2 * Q_BLOCK), :]
        vw = vd_ref[pl.ds(krow, 2 * Q_BLOCK), :]
        s = lax.dot_general(q, kw, (((1,), (1,)), ((), ())), preferred_element_type=F32)
        ok = cur_ok | (prev_ok & (n > 0))
        s = jnp.where(ok, s, NEG_INF)
        m = jnp.max(s, axis=1, keepdims=True)
        p = jnp.exp(s - m)
        ssum = jnp.sum(p, axis=1, keepdims=True)
        acc = jnp.dot(p.astype(BF16), vw, preferred_element_type=F32)
        start = rho + n * (Q_BLOCK * r)
        if r == 1:
            rows = pl.ds(pl.multiple_of(start, Q_BLOCK), Q_BLOCK)
        else:
            rows = pl.ds(start, Q_BLOCK, stride=r)
        acc_ref[rows, :] = acc
        m_ref[rows, :] = jnp.broadcast_to(m, (Q_BLOCK, HEAD_DIM))
        s_ref[rows, :] = jnp.broadcast_to(ssum, (Q_BLOCK, HEAD_DIM))
        return 0

    lax.fori_loop(0, r * nb, block, 0)


def _attn_kernel(seq, q_ref, k_ref, v_ref, cos_ref, sin_ref, o_ref,
                 nat_ref, qd_ref, kd_ref, vd_ref, num_ref, den_ref, max_ref, acc_ref, m_ref, s_ref):
    g = pl.program_id(2)
    n_chunks = seq // ROW_CHUNK
    common = (q_ref, k_ref, v_ref, cos_ref, sin_ref, nat_ref, qd_ref, kd_ref, vd_ref)

    @pl.when(g == 0)
    def _():
        _attn_group(ATTN_DILATIONS[0], seq, *common, num_ref, max_ref, den_ref)

    def merge(last):
        def body(c, _):
            rows = pl.ds(pl.multiple_of(c * ROW_CHUNK, ROW_CHUNK), ROW_CHUNK)
            m_old = max_ref[rows, :]
            m_grp = m_ref[rows, :]
            m_new = jnp.maximum(m_old, m_grp)
            a = jnp.exp(m_old - m_new)
            b = jnp.exp(m_grp - m_new)
            num = num_ref[rows, :] * a + acc_ref[rows, :] * b
            den = den_ref[rows, :] * a + s_ref[rows, :] * b
            if last:
                o_ref[rows, :] = (num / den).astype(o_ref.dtype)
            else:
                num_ref[rows, :] = num
                den_ref[rows, :] = den
                max_ref[rows, :] = m_new
            return 0
        lax.fori_loop(0, n_chunks, body, 0)

    for gi in (1, 2):
        @pl.when(g == gi)
        def _(gi=gi):
            _attn_group(ATTN_DILATIONS[gi], seq, *common, acc_ref, m_ref, s_ref)
            merge(last=(gi == N_GROUPS - 1))


def _attention(proj, cos2, sins, batch, seq):
    T = proj.shape[0]
    qb, kb, vb = COL_Q // HEAD_DIM, COL_K // HEAD_DIM, COL_VA // HEAD_DIM
    pad_rows = seq + max(ATTN_DILATIONS) * Q_BLOCK

    def head_spec(base):
        return pl.BlockSpec((seq, HEAD_DIM), lambda b, j, g: (b, base + g * HEADS_PER_GROUP + j))

    tab_spec = pl.BlockSpec((seq, HEAD_DIM), lambda b, j, g: (b, 0))
    big = pltpu.VMEM((seq, HEAD_DIM), F32)
    return pl.pallas_call(
        functools.partial(_attn_kernel, seq),
        grid=(batch, HEADS_PER_GROUP, N_GROUPS),
        in_specs=[head_spec(qb), head_spec(kb), head_spec(vb), tab_spec, tab_spec],
        out_specs=pl.BlockSpec((seq, HEAD_DIM), lambda b, j, g: (b, j)),
        out_shape=jax.ShapeDtypeStruct((T, ATTN_OUT_WIDTH), BF16),
        scratch_shapes=[big,
                        pltpu.VMEM((seq, HEAD_DIM), BF16),
                        pltpu.VMEM((pad_rows, HEAD_DIM), BF16),
                        pltpu.VMEM((pad_rows, HEAD_DIM), BF16),
                        big, big, big, big, big, big],
        compiler_params=_params(("arbitrary", "arbitrary", "arbitrary"), 48),
        name="dilated_attn",
    )(proj, proj, proj, cos2, sins)


def _mix_kernel(seq, ts,
                b_ref, c_ref, v_ref, ch_ref, vh_ref, gc_ref, ga_ref, ao_ref, x_ref,
                cw_ref, wc_ref, wa_ref, wo_ref, gate1_ref, g2_ref, sc2_ref, sh2_ref, wr_ref,
                x1_ref, h2_ref, lt_ref):
    i = pl.program_id(0)
    not_start = ((i * ts) % seq != 0).astype(F32)
    u = c_ref[...].astype(F32) * v_ref[...].astype(F32)
    halo = ch_ref[...].astype(F32) * vh_ref[...].astype(F32) * not_start
    h1 = halo[15:16, :]
    h2 = halo[14:15, :]
    row = lax.broadcasted_iota(jnp.int32, u.shape, 0)
    up1 = jnp.where(row == 0, h1, pltpu.roll(u, 1, 0))
    up2 = jnp.where(row == 0, h2, jnp.where(row == 1, h1, pltpu.roll(u, 2, 0)))
    cw = cw_ref[...]
    conv = cw[0:1, :] * up2 + cw[1:2, :] * up1 + cw[2:3, :] * u
    yc = jnp.dot((b_ref[...].astype(F32) * conv).astype(BF16), wc_ref[...], preferred_element_type=F32)
    ya = jnp.dot(ao_ref[...], wa_ref[...], preferred_element_type=F32)
    merged = _sigmoid(gc_ref[...].astype(F32)) * yc + _sigmoid(ga_ref[...].astype(F32)) * ya
    mo = jnp.dot(merged.astype(BF16), wo_ref[...], preferred_element_type=F32)
    x1 = x_ref[...] + gate1_ref[...] * mo
    x1_ref[...] = x1
    h2n = _modnorm(x1, g2_ref[...], sc2_ref[...], sh2_ref[...])
    h2_ref[...] = h2n.astype(BF16)
    lt_ref[...] = lax.dot_general(wr_ref[...], h2n, (((1,), (1,)), ((), ())),
                                  precision=lax.Precision.HIGHEST, preferred_element_type=F32)


def _mixer_out(proj, attn_o, x2d, conv_w, wc, wa, wo, mod3, g2, wr_t, seq):
    T, D = x2d.shape
    ts = 256
    per_seq = seq // ts
    halo_rows = 16

    def col(width, off):
        return pl.BlockSpec((ts, width), lambda i: (i, off // width))

    def halo(off):
        return pl.BlockSpec((halo_rows, CONV_WIDTH),
                            lambda i: (jnp.maximum(i * (ts // halo_rows) - 1, 0), off // CONV_WIDTH))

    def const(shape):
        return pl.BlockSpec(shape, lambda i: (0,) * len(shape), pipeline_mode=pl.Buffered(1))

    def mod(k):
        return pl.BlockSpec((None, 1, D), lambda i: ((i // per_seq) * 6 + k, 0, 0))

    return pl.pallas_call(
        functools.partial(_mix_kernel, seq, ts),
        grid=(T // ts,),
        in_specs=[col(CONV_WIDTH, COL_B), col(CONV_WIDTH, COL_C), col(CONV_WIDTH, COL_V),
                  halo(COL_C), halo(COL_V),
                  col(D, COL_GCONV), col(D, COL_GATTN),
                  pl.BlockSpec((ts, ATTN_OUT_WIDTH), lambda i: (i, 0)),
                  pl.BlockSpec((ts, D), lambda i: (i, 0)),
                  const((CONV_K, CONV_WIDTH)), const((CONV_WIDTH, D)), const((ATTN_OUT_WIDTH, D)),
                  const((D, D)),
                  mod(2), const((1, D)), mod(4), mod(3), const((N_EXPERTS, D))],
        out_specs=[pl.BlockSpec((ts, D), lambda i: (i, 0)),
                   pl.BlockSpec((ts, D), lambda i: (i, 0)),
                   pl.BlockSpec((N_EXPERTS, ts), lambda i: (0, i))],
        out_shape=[jax.ShapeDtypeStruct((T, D), F32),
                   jax.ShapeDtypeStruct((T, D), BF16),
                   jax.ShapeDtypeStruct((N_EXPERTS, T), F32)],
        compiler_params=_params(("arbitrary",), 56),
        name="mixer_out",
    )(proj, proj, proj, proj, proj, proj, proj, attn_o, x2d,
      conv_w, wc, wa, wo, mod3, g2, mod3, mod3, wr_t)


def _route_kernel(lt_ref, bias_ref, g_ref):
    lt = lt_ref[...]
    tr = lt.shape[1]
    per = N_EXPERTS // N_EXPERT_GROUPS
    scores = _sigmoid(lt)
    sel = scores + bias_ref[...]
    sel3 = sel.reshape(N_EXPERT_GROUPS, per, tr)
    sub = lax.broadcasted_iota(jnp.int32, sel3.shape, 1)
    top1 = jnp.max(sel3, axis=1, keepdims=True)
    first = jnp.min(jnp.where(sel3 == top1, sub, per), axis=1, keepdims=True)
    top2 = jnp.max(jnp.where(sub == first, -jnp.inf, sel3), axis=1, keepdims=True)
    gs = (top1 + top2).reshape(N_EXPERT_GROUPS, tr)
    gidx = lax.broadcasted_iota(jnp.int32, gs.shape, 0)
    rank = jnp.zeros(gs.shape, jnp.int32)
    for other in range(N_EXPERT_GROUPS):
        row = gs[other:other + 1, :]
        rank += ((row > gs) | ((row == gs) & (other < gidx))).astype(jnp.int32)
    gmask = rank < TOP_GROUPS
    emask = jnp.broadcast_to(gmask[:, None, :], sel3.shape).reshape(N_EXPERTS, tr)
    cand = jnp.where(emask, sel, NEG_INF)
    eidx = lax.broadcasted_iota(jnp.int32, cand.shape, 0)
    chosen = jnp.zeros(cand.shape, jnp.bool_)
    for _ in range(TOP_K):
        mx = jnp.max(cand, axis=0, keepdims=True)
        hit = eidx == jnp.min(jnp.where(cand == mx, eidx, N_EXPERTS), axis=0, keepdims=True)
        chosen = chosen | hit
        cand = jnp.where(hit, -jnp.inf, cand)
    w = jnp.where(chosen, scores, 0.0)
    gates = w / jnp.sum(w, axis=0, keepdims=True) * ROUTED_SCALE
    g_ref[...] = gates.T


def _route(logits_t, router_bias):
    E, T = logits_t.shape
    tr = 512
    return pl.pallas_call(
        _route_kernel,
        grid=(T // tr,),
        in_specs=[pl.BlockSpec((E, tr), lambda i: (0, i)),
                  pl.BlockSpec((E, 1), lambda i: (0, 0))],
        out_specs=pl.BlockSpec((tr, E), lambda i: (i, 0)),
        out_shape=jax.ShapeDtypeStruct((T, E), F32),
        compiler_params=_params(("arbitrary",), 32),
        name="router",
    )(logits_t, router_bias.reshape(E, 1))


def _swiglu(t, wg, wu, wd, row_scale=None):
    a = jnp.dot(t, wg, preferred_element_type=F32)
    u = jnp.dot(t, wu, preferred_element_type=F32)
    h = (a * _sigmoid(a)) * u
    if row_scale is not None:
        h = h * row_scale
    return jnp.dot(h.astype(BF16), wd, preferred_element_type=F32)


def _moe_kernel(h_ref, gates_ref, wg_ref, wu_ref, wd_ref, sg_ref, su_ref, sd_ref,
                x1_ref, gate2_ref, gf_ref, y_ref, acc_ref):
    e = pl.program_id(1)
    t = h_ref[...]

    @pl.when(e == 0)
    def _():
        acc_ref[...] = _swiglu(t, sg_ref[...], su_ref[...], sd_ref[...])

    gates = gates_ref[...]
    lane = lax.broadcasted_iota(jnp.int32, gates.shape, 1)
    g = jnp.sum(jnp.where(lane == e, gates, 0.0), axis=1, keepdims=True)
    acc_ref[...] += g * _swiglu(t, wg_ref[...], wu_ref[...], wd_ref[...])

    @pl.when(e == N_EXPERTS - 1)
    def _():
        x2 = x1_ref[...] + gate2_ref[...] * acc_ref[...]
        xf = x2 * lax.rsqrt(jnp.mean(x2 * x2, axis=-1, keepdims=True) + EPS)
        y_ref[...] = xf * gf_ref[...]


def _moe(h2, gates, wg, wu, wd, sg, su, sd, x1, mod3, gf, seq):
    T, D = h2.shape
    tm = 512
    per_seq = seq // tm

    def const(shape):
        return pl.BlockSpec(shape, lambda i, e: (0,) * len(shape))

    return pl.pallas_call(
        _moe_kernel,
        grid=(T // tm, N_EXPERTS),
        in_specs=[pl.BlockSpec((tm, D), lambda i, e: (i, 0)),
                  pl.BlockSpec((tm, N_EXPERTS), lambda i, e: (i, 0)),
                  pl.BlockSpec((None, D, D_EXPERT), lambda i, e: (e, 0, 0)),
                  pl.BlockSpec((None, D, D_EXPERT), lambda i, e: (e, 0, 0)),
                  pl.BlockSpec((None, D_EXPERT, D), lambda i, e: (e, 0, 0)),
                  const((D, D_EXPERT)), const((D, D_EXPERT)), const((D_EXPERT, D)),
                  pl.BlockSpec((tm, D), lambda i, e: (i, 0)),
                  pl.BlockSpec((None, 1, D), lambda i, e: ((i // per_seq) * 6 + 5, 0, 0)),
                  const((1, D))],
        out_specs=pl.BlockSpec((tm, D), lambda i, e: (i, 0)),
        out_shape=jax.ShapeDtypeStruct((T, D), F32),
        scratch_shapes=[pltpu.VMEM((tm, D), F32)],
        compiler_params=_params(("arbitrary", "arbitrary"), 56),
        name="moe_dense",
    )(h2, gates, wg, wu, wd, sg, su, sd, x1, mod3, gf)


def kernel(x, c, positions, norm_mix_g, w_ada, b_ada, w_in, conv_w, w_conv_out, w_attn_out,
           w_o, norm_ffn_g, w_router, router_bias, w_exp_gate, w_exp_up, w_exp_down,
           w_sh_gate, w_sh_up, w_sh_down, norm_final_g):
    B, S, D = x.shape
    T = B * S
    assert w_ada.shape[0] == 1, "the final norm is fused into the single layer's expert kernel"
    l = 0
    x2d = x.reshape(T, D)
    mod3 = _ada(c, w_ada[l], b_ada[l]).reshape(B * 6, 1, D)
    h1 = _norm1(x2d, norm_mix_g[l].reshape(1, D), mod3, S)
    proj = _inproj(h1, w_in[l])
    cos2, sins = _rope_tables(positions)
    attn_o = _attention(proj, cos2, sins, B, S)
    x1, h2, logits_t = _mixer_out(
        proj, attn_o, x2d, conv_w[l], w_conv_out[l].astype(BF16), w_attn_out[l].astype(BF16),
        w_o[l].astype(BF16), mod3, norm_ffn_g[l].reshape(1, D), w_router[l].T, S)
    gates = _route(logits_t, router_bias[l])
    y = _moe(h2, gates, w_exp_gate[l].astype(BF16), w_exp_up[l].astype(BF16),
             w_exp_down[l].astype(BF16), w_sh_gate[l].astype(BF16), w_sh_up[l].astype(BF16),
             w_sh_down[l].astype(BF16), x1, mod3, norm_final_g.reshape(1, D), S)
    return y.reshape(B, S, D)
```

```python
import functools

import jax
import jax.numpy as jnp
from jax import lax
from jax.experimental import pallas as pl
from jax.experimental.pallas import tpu as pltpu

D_MODEL = 2048
HEAD_DIM = 128
ATTN_DILATIONS = (1, 4, 16)
N_GROUPS = 3
HEADS_PER_GROUP = 4
ATTN_WIDTH = N_GROUPS * HEADS_PER_GROUP * HEAD_DIM
ATTN_OUT_WIDTH = HEADS_PER_GROUP * HEAD_DIM
ROPE_THETA = 10000.0
Q_BLOCK = 128
CONV_WIDTH = D_MODEL // 2
CONV_K = 3
IN_COLS = 3 * CONV_WIDTH + 3 * ATTN_WIDTH + 2 * D_MODEL
N_EXPERTS = 64
TOP_K = 8
N_EXPERT_GROUPS = 8
TOP_GROUPS = 4
D_EXPERT = D_MODEL // 4
ROUTED_SCALE = 2.5
EPS = 1e-6
NEG_INF = -1e30

F32 = jnp.float32
BF16 = jnp.bfloat16
U32 = jnp.uint32
MIB = 1024 * 1024

PROJ_BLOCK = 512
N_PROJ_BLOCKS = IN_COLS // PROJ_BLOCK
GATE_SRC_BLOCK = (3 * CONV_WIDTH + 3 * ATTN_WIDTH) // PROJ_BLOCK
N_GATE_BLOCKS = 2 * D_MODEL // PROJ_BLOCK
COL_GCONV = 0
COL_GATTN = D_MODEL
COL_B = 2 * D_MODEL
COL_C = COL_B + CONV_WIDTH
COL_V = COL_C + CONV_WIDTH
COL_Q = COL_V + CONV_WIDTH
COL_K = COL_Q + ATTN_WIDTH
COL_VA = COL_K + ATTN_WIDTH


def _params(semantics, vmem_mib):
    return pltpu.CompilerParams(dimension_semantics=semantics, vmem_limit_bytes=vmem_mib * MIB)


def _sigmoid(x):
    return 1.0 / (1.0 + jnp.exp(-x))


def _pack_halves(x):
    n = x.shape[1] // 2
    return pltpu.pack_elementwise([x[:, :n], x[:, n:]], packed_dtype=BF16)


def _unpack_halves(xp):
    lo = pltpu.unpack_elementwise(xp, index=0, packed_dtype=BF16, unpacked_dtype=F32)
    hi = pltpu.unpack_elementwise(xp, index=1, packed_dtype=BF16, unpacked_dtype=F32)
    return lo, hi


def _unpack_rows_bf16(xp):
    lo, hi = _unpack_halves(xp)
    return jnp.concatenate([lo.astype(BF16), hi.astype(BF16)], axis=1)


def _ada_kernel(c_ref, w_ref, b_ref, o_ref):
    c = c_ref[...]
    s = c * _sigmoid(c)
    o_ref[...] = jnp.dot(s, w_ref[...], precision=lax.Precision.HIGHEST,
                         preferred_element_type=F32) + b_ref[...]


def _ada(c, w_ada, b_ada):
    B, D = c.shape
    N = w_ada.shape[1]
    tn = 1024
    return pl.pallas_call(
        _ada_kernel,
        grid=(N // tn,),
        in_specs=[pl.BlockSpec((B, D), lambda j: (0, 0)),
                  pl.BlockSpec((D, tn), lambda j: (0, j)),
                  pl.BlockSpec((1, tn), lambda j: (0, j))],
        out_specs=pl.BlockSpec((B, tn), lambda j: (0, j)),
        out_shape=jax.ShapeDtypeStruct((B, N), F32),
        compiler_params=_params(("arbitrary",), 40),
        name="ada_mod",
    )(c, w_ada, b_ada.reshape(1, N))


def _modnorm(x, g, scale, shift):
    xf = x * lax.rsqrt(jnp.mean(x * x, axis=-1, keepdims=True) + EPS)
    return (xf * g) * (1.0 + scale) + shift


def _norm1_kernel(x_ref, g_ref, sc_ref, sh_ref, o_ref):
    o_ref[...] = _modnorm(x_ref[...], g_ref[...], sc_ref[...], sh_ref[...]).astype(o_ref.dtype)


def _norm1(x2d, g, mod3, seq):
    T, D = x2d.shape
    tm = 512
    per_seq = seq // tm
    return pl.pallas_call(
        _norm1_kernel,
        grid=(T // tm,),
        in_specs=[pl.BlockSpec((tm, D), lambda i: (i, 0)),
                  pl.BlockSpec((1, D), lambda i: (0, 0)),
                  pl.BlockSpec((None, 1, D), lambda i: ((i // per_seq) * 6 + 1, 0, 0)),
                  pl.BlockSpec((None, 1, D), lambda i: ((i // per_seq) * 6 + 0, 0, 0))],
        out_specs=pl.BlockSpec((tm, D), lambda i: (i, 0)),
        out_shape=jax.ShapeDtypeStruct((T, D), BF16),
        compiler_params=_params(("arbitrary",), 32),
        name="norm_mix",
    )(x2d, g, mod3, mod3)


def _inproj_kernel(h_ref, w_ref, o_ref, wbf_ref):
    @pl.when(pl.program_id(1) == 0)
    def _():
        wbf_ref[...] = w_ref[...].astype(BF16)

    o_ref[...] = jnp.dot(h_ref[...], wbf_ref[...], preferred_element_type=F32).astype(o_ref.dtype)


def _proj_dst_block(j):
    return jnp.where(j < GATE_SRC_BLOCK, j + N_GATE_BLOCKS, j - GATE_SRC_BLOCK)


def _inproj(h, w_in):
    T, D = h.shape
    tm = 2048
    tn = PROJ_BLOCK
    return pl.pallas_call(
        _inproj_kernel,
        grid=(N_PROJ_BLOCKS, T // tm),
        in_specs=[pl.BlockSpec((tm, D), lambda j, i: (i, 0)),
                  pl.BlockSpec((D, tn), lambda j, i: (0, j))],
        out_specs=pl.BlockSpec((tm, tn), lambda j, i: (i, _proj_dst_block(j))),
        out_shape=jax.ShapeDtypeStruct((T, IN_COLS), BF16),
        scratch_shapes=[pltpu.VMEM((D, tn), BF16)],
        compiler_params=_params(("arbitrary", "arbitrary"), 48),
        name="in_proj",
    )(h, w_in)


def _rope_kernel(pos_ref, inv_ref, sign_ref, cos_ref, sin_ref):
    ang = pos_ref[...].astype(F32) * inv_ref[...]
    cos_ref[...] = jnp.cos(ang)
    sin_ref[...] = jnp.sin(ang) * sign_ref[...]


def _rope_tables(positions):
    T = positions.size
    ts = 1024
    inv = ROPE_THETA ** (-jnp.arange(0, HEAD_DIM, 2, dtype=F32) / HEAD_DIM)
    inv2 = jnp.concatenate([inv, inv]).reshape(1, HEAD_DIM)
    sign = jnp.concatenate([-jnp.ones((HEAD_DIM // 2,), F32), jnp.ones((HEAD_DIM // 2,), F32)]).reshape(1, HEAD_DIM)
    return pl.pallas_call(
        _rope_kernel,
        grid=(T // ts,),
        in_specs=[pl.BlockSpec((ts, 1), lambda i: (i, 0)),
                  pl.BlockSpec((1, HEAD_DIM), lambda i: (0, 0)),
                  pl.BlockSpec((1, HEAD_DIM), lambda i: (0, 0))],
        out_specs=[pl.BlockSpec((ts, HEAD_DIM), lambda i: (i, 0)),
                   pl.BlockSpec((ts, HEAD_DIM), lambda i: (i, 0))],
        out_shape=[jax.ShapeDtypeStruct((T, HEAD_DIM), F32)] * 2,
        compiler_params=_params(("arbitrary",), 32),
        name="rope_tables",
    )(positions.reshape(T, 1), inv2, sign)


ROW_CHUNK = 512


def _attn_group(r, seq, q_ref, k_ref, v_ref, cos_ref, sin_ref, nat_ref, qd_ref, kd_ref, vd_ref,
                acc_ref, m_ref, s_ref):
    n_sub = seq // r
    nb = n_sub // Q_BLOCK
    cls = n_sub + Q_BLOCK
    n_chunks = seq // ROW_CHUNK
    scale = HEAD_DIM ** -0.5

    def rope(t_ref, c0, mult):
        t = t_ref[pl.ds(c0, ROW_CHUNK), :].astype(F32)
        cs = cos_ref[pl.ds(c0, ROW_CHUNK), :]
        sn = sin_ref[pl.ds(c0, ROW_CHUNK), :]
        out = t * cs + pltpu.roll(t, HEAD_DIM // 2, 1) * sn
        return out * mult if mult is not None else out

    def deinterleave(dst_ref, dst_stride, dst_off):
        for rho in range(r):
            dst_ref[pl.ds(rho * dst_stride + dst_off, n_sub), :] = (
                nat_ref[pl.ds(rho, n_sub, stride=r), :].astype(BF16))

    zeros_blk = jnp.zeros((Q_BLOCK, HEAD_DIM), BF16)
    for rho in range(r):
        kd_ref[pl.ds(rho * cls, Q_BLOCK), :] = zeros_blk
        vd_ref[pl.ds(rho * cls, Q_BLOCK), :] = zeros_blk

    if r == 1:
        def fill(c, _):
            c0 = pl.multiple_of(c * ROW_CHUNK, ROW_CHUNK)
            qd_ref[pl.ds(c0, ROW_CHUNK), :] = rope(q_ref, c0, scale).astype(BF16)
            kd_ref[pl.ds(Q_BLOCK + c0, ROW_CHUNK), :] = rope(k_ref, c0, None).astype(BF16)
            vd_ref[pl.ds(Q_BLOCK + c0, ROW_CHUNK), :] = v_ref[pl.ds(c0, ROW_CHUNK), :]
            return 0
        lax.fori_loop(0, n_chunks, fill, 0)
    else:
        def fill_q(c, _):
            c0 = pl.multiple_of(c * ROW_CHUNK, ROW_CHUNK)
            nat_ref[pl.ds(c0, ROW_CHUNK), :] = rope(q_ref, c0, scale)
            return 0
        lax.fori_loop(0, n_chunks, fill_q, 0)
        deinterleave(qd_ref, n_sub, 0)

        def fill_k(c, _):
            c0 = pl.multiple_of(c * ROW_CHUNK, ROW_CHUNK)
            nat_ref[pl.ds(c0, ROW_CHUNK), :] = rope(k_ref, c0, None)
            return 0
        lax.fori_loop(0, n_chunks, fill_k, 0)
        deinterleave(kd_ref, cls, Q_BLOCK)

        def fill_v(c, _):
            c0 = pl.multiple_of(c * ROW_CHUNK, ROW_CHUNK)
            nat_ref[pl.ds(c0, ROW_CHUNK), :] = v_ref[pl.ds(c0, ROW_CHUNK), :].astype(F32)
            return 0
        lax.fori_loop(0, n_chunks, fill_v, 0)
        deinterleave(vd_ref, cls, Q_BLOCK)

    qi = lax.broadcasted_iota(jnp.int32, (Q_BLOCK, 2 * Q_BLOCK), 0)
    kj = lax.broadcasted_iota(jnp.int32, (Q_BLOCK, 2 * Q_BLOCK), 1)
    cur_ok = (kj >= Q_BLOCK) & (kj - Q_BLOCK <= qi)
    prev_ok = (kj < Q_BLOCK) & (kj >= qi)

    def block(blk, _):
        rho = blk // nb
        n = blk % nb
        qrow = pl.multiple_of(rho * n_sub + n * Q_BLOCK, Q_BLOCK)
        krow = pl.multiple_of(rho * cls + n * Q_BLOCK, Q_BLOCK)
        q = qd_ref[pl.ds(qrow, Q_BLOCK), :]
        kw = kd_ref[pl.ds(krow, 2 * Q_BLOCK), :]
        vw = vd_ref[pl.ds(krow, 2 * Q_BLOCK), :]
        s = lax.dot_general(q, kw, (((1,), (1,)), ((), ())), preferred_element_type=F32)
        ok = cur_ok | (prev_ok & (n > 0))
        s = jnp.where(ok, s, NEG_INF)
        m = jnp.max(s, axis=1, keepdims=True)
        p = jnp.exp(s - m)
        ssum = jnp.sum(p, axis=1, keepdims=True)
        acc = jnp.dot(p.astype(BF16), vw, preferred_element_type=F32)
        start = rho + n * (Q_BLOCK * r)
        if r == 1:
            rows = pl.ds(pl.multiple_of(start, Q_BLOCK), Q_BLOCK)
        else:
            rows = pl.ds(start, Q_BLOCK, stride=r)
        acc_ref[rows, :] = acc
        m_ref[rows, :] = jnp.broadcast_to(m, (Q_BLOCK, HEAD_DIM))
        s_ref[rows, :] = jnp.broadcast_to(ssum, (Q_BLOCK, HEAD_DIM))
        return 0

    lax.fori_loop(0, r * nb, block, 0)


def _attn_kernel(seq, q_ref, k_ref, v_ref, cos_ref, sin_ref, o_ref,
                 nat_ref, qd_ref, kd_ref, vd_ref, num_ref, den_ref, max_ref, acc_ref, m_ref, s_ref):
    g = pl.program_id(2)
    n_chunks = seq // ROW_CHUNK
    common = (q_ref, k_ref, v_ref, cos_ref, sin_ref, nat_ref, qd_ref, kd_ref, vd_ref)

    @pl.when(g == 0)
    def _():
        _attn_group(ATTN_DILATIONS[0], seq, *common, num_ref, max_ref, den_ref)

    def merge(last):
        def body(c, _):
            rows = pl.ds(pl.multiple_of(c * ROW_CHUNK, ROW_CHUNK), ROW_CHUNK)
            m_old = max_ref[rows, :]
            m_grp = m_ref[rows, :]
            m_new = jnp.maximum(m_old, m_grp)
            a = jnp.exp(m_old - m_new)
            b = jnp.exp(m_grp - m_new)
            num = num_ref[rows, :] * a + acc_ref[rows, :] * b
            den = den_ref[rows, :] * a + s_ref[rows, :] * b
            if last:
                o_ref[rows, :] = (num / den).astype(o_ref.dtype)
            else:
                num_ref[rows, :] = num
                den_ref[rows, :] = den
                max_ref[rows, :] = m_new
            return 0
        lax.fori_loop(0, n_chunks, body, 0)

    for gi in (1, 2):
        @pl.when(g == gi)
        def _(gi=gi):
            _attn_group(ATTN_DILATIONS[gi], seq, *common, acc_ref, m_ref, s_ref)
            merge(last=(gi == N_GROUPS - 1))


def _attention(proj, cos2, sins, batch, seq):
    T = proj.shape[0]
    qb, kb, vb = COL_Q // HEAD_DIM, COL_K // HEAD_DIM, COL_VA // HEAD_DIM
    pad_rows = seq + max(ATTN_DILATIONS) * Q_BLOCK

    def head_spec(base):
        return pl.BlockSpec((seq, HEAD_DIM), lambda b, j, g: (b, base + g * HEADS_PER_GROUP + j))

    tab_spec = pl.BlockSpec((seq, HEAD_DIM), lambda b, j, g: (b, 0))
    big = pltpu.VMEM((seq, HEAD_DIM), F32)
    return pl.pallas_call(
        functools.partial(_attn_kernel, seq),
        grid=(batch, HEADS_PER_GROUP, N_GROUPS),
        in_specs=[head_spec(qb), head_spec(kb), head_spec(vb), tab_spec, tab_spec],
        out_specs=pl.BlockSpec((seq, HEAD_DIM), lambda b, j, g: (b, j)),
        out_shape=jax.ShapeDtypeStruct((T, ATTN_OUT_WIDTH), BF16),
        scratch_shapes=[big,
                        pltpu.VMEM((seq, HEAD_DIM), BF16),
                        pltpu.VMEM((pad_rows, HEAD_DIM), BF16),
                        pltpu.VMEM((pad_rows, HEAD_DIM), BF16),
                        big, big, big, big, big, big],
        compiler_params=_params(("arbitrary", "arbitrary", "arbitrary"), 48),
        name="dilated_attn",
    )(proj, proj, proj, cos2, sins)


def _mix_kernel(seq, ts,
                b_ref, c_ref, v_ref, ch_ref, vh_ref, gc_ref, ga_ref, ao_ref, x_ref,
                cw_ref, wc_ref, wa_ref, wo_ref, gate1_ref, g2_ref, sc2_ref, sh2_ref, wr_ref,
                x1_ref, h2_ref, lt_ref):
    i = pl.program_id(0)
    not_start = ((i * ts) % seq != 0).astype(F32)
    u = c_ref[...].astype(F32) * v_ref[...].astype(F32)
    halo = ch_ref[...].astype(F32) * vh_ref[...].astype(F32) * not_start
    h1 = halo[15:16, :]
    h2 = halo[14:15, :]
    row = lax.broadcasted_iota(jnp.int32, u.shape, 0)
    up1 = jnp.where(row == 0, h1, pltpu.roll(u, 1, 0))
    up2 = jnp.where(row == 0, h2, jnp.where(row == 1, h1, pltpu.roll(u, 2, 0)))
    cw = cw_ref[...]
    conv = cw[0:1, :] * up2 + cw[1:2, :] * up1 + cw[2:3, :] * u
    yc = jnp.dot((b_ref[...].astype(F32) * conv).astype(BF16), wc_ref[...], preferred_element_type=F32)
    ya = jnp.dot(ao_ref[...], wa_ref[...], preferred_element_type=F32)
    merged = _sigmoid(gc_ref[...].astype(F32)) * yc + _sigmoid(ga_ref[...].astype(F32)) * ya
    mo = jnp.dot(merged.astype(BF16), wo_ref[...], preferred_element_type=F32)
    x1 = x_ref[...] + gate1_ref[...] * mo
    x1_ref[...] = x1
    h2n = _modnorm(x1, g2_ref[...], sc2_ref[...], sh2_ref[...])
    h2_ref[...] = _pack_halves(h2n)
    lt_ref[...] = lax.dot_general(wr_ref[...], h2n, (((1,), (1,)), ((), ())),
                                  precision=lax.Precision.HIGHEST, preferred_element_type=F32)


def _mixer_out(proj, attn_o, x2d, conv_w, wc, wa, wo, mod3, g2, wr_t, seq):
    T, D = x2d.shape
    ts = 256
    per_seq = seq // ts
    halo_rows = 16

    def col(width, off):
        return pl.BlockSpec((ts, width), lambda i: (i, off // width))

    def halo(off):
        return pl.BlockSpec((halo_rows, CONV_WIDTH),
                            lambda i: (jnp.maximum(i * (ts // halo_rows) - 1, 0), off // CONV_WIDTH))

    def const(shape):
        return pl.BlockSpec(shape, lambda i: (0,) * len(shape), pipeline_mode=pl.Buffered(1))

    def mod(k):
        return pl.BlockSpec((None, 1, D), lambda i: ((i // per_seq) * 6 + k, 0, 0))

    return pl.pallas_call(
        functools.partial(_mix_kernel, seq, ts),
        grid=(T // ts,),
        in_specs=[col(CONV_WIDTH, COL_B), col(CONV_WIDTH, COL_C), col(CONV_WIDTH, COL_V),
                  halo(COL_C), halo(COL_V),
                  col(D, COL_GCONV), col(D, COL_GATTN),
                  pl.BlockSpec((ts, ATTN_OUT_WIDTH), lambda i: (i, 0)),
                  pl.BlockSpec((ts, D), lambda i: (i, 0)),
                  const((CONV_K, CONV_WIDTH)), const((CONV_WIDTH, D)), const((ATTN_OUT_WIDTH, D)),
                  const((D, D)),
                  mod(2), const((1, D)), mod(4), mod(3), const((N_EXPERTS, D))],
        out_specs=[pl.BlockSpec((ts, D), lambda i: (i, 0)),
                   pl.BlockSpec((ts, D // 2), lambda i: (i, 0)),
                   pl.BlockSpec((N_EXPERTS, ts), lambda i: (0, i))],
        out_shape=[jax.ShapeDtypeStruct((T, D), F32),
                   jax.ShapeDtypeStruct((T, D // 2), U32),
                   jax.ShapeDtypeStruct((N_EXPERTS, T), F32)],
        compiler_params=_params(("arbitrary",), 56),
        name="mixer_out",
    )(proj, proj, proj, proj, proj, proj, proj, attn_o, x2d,
      conv_w, wc, wa, wo, mod3, g2, mod3, mod3, wr_t)


def _route_kernel(lt_ref, bias_ref, eid_ref, rank_ref, w_ref, cnt_ref, carry_ref):
    @pl.when(pl.program_id(0) == 0)
    def _():
        carry_ref[...] = jnp.zeros_like(carry_ref)

    lt = lt_ref[...]
    tr = lt.shape[1]
    per = N_EXPERTS // N_EXPERT_GROUPS
    scores = _sigmoid(lt)
    sel = scores + bias_ref[...]
    sel3 = sel.reshape(N_EXPERT_GROUPS, per, tr)
    sub = lax.broadcasted_iota(jnp.int32, sel3.shape, 1)
    top1 = jnp.max(sel3, axis=1, keepdims=True)
    first = jnp.min(jnp.where(sel3 == top1, sub, per), axis=1, keepdims=True)
    top2 = jnp.max(jnp.where(sub == first, -jnp.inf, sel3), axis=1, keepdims=True)
    gs = (top1 + top2).reshape(N_EXPERT_GROUPS, tr)
    gidx = lax.broadcasted_iota(jnp.int32, gs.shape, 0)
    grank = jnp.zeros(gs.shape, jnp.int32)
    for other in range(N_EXPERT_GROUPS):
        row = gs[other:other + 1, :]
        grank += ((row > gs) | ((row == gs) & (other < gidx))).astype(jnp.int32)
    gmask = grank < TOP_GROUPS
    emask = jnp.broadcast_to(gmask[:, None, :], sel3.shape).reshape(N_EXPERTS, tr)
    cand = jnp.where(emask, sel, NEG_INF)
    eidx = lax.broadcasted_iota(jnp.int32, cand.shape, 0)
    hits, eids, ws = [], [], []
    for _ in range(TOP_K):
        mx = jnp.max(cand, axis=0, keepdims=True)
        idx = jnp.min(jnp.where(cand == mx, eidx, N_EXPERTS), axis=0, keepdims=True)
        hit = eidx == idx
        hits.append(hit)
        eids.append(idx)
        ws.append(jnp.sum(jnp.where(hit, scores, 0.0), axis=0, keepdims=True))
        cand = jnp.where(hit, -jnp.inf, cand)
    chosen = hits[0]
    for hit in hits[1:]:
        chosen = chosen | hit
    chosen = chosen.astype(F32)
    before = (lax.broadcasted_iota(jnp.int32, (tr, tr), 0) < lax.broadcasted_iota(jnp.int32, (tr, tr), 1))
    seen = jnp.dot(chosen.astype(BF16), before.astype(BF16), preferred_element_type=F32) + carry_ref[...]
    ranks = [jnp.sum(jnp.where(hit, seen, 0.0), axis=0, keepdims=True) for hit in hits]
    carry_ref[...] += jnp.sum(chosen, axis=1, keepdims=True)
    cnt_ref[...] = carry_ref[...].astype(jnp.int32)
    wsum = ws[0]
    for w in ws[1:]:
        wsum = wsum + w
    w8 = jnp.concatenate(ws, axis=0) / wsum * ROUTED_SCALE
    w_ref[...] = w8.T
    eid_ref[...] = jnp.concatenate(eids, axis=0)
    rank_ref[...] = jnp.concatenate(ranks, axis=0).astype(jnp.int32)


def _route(logits_t, router_bias):
    E, T = logits_t.shape
    tr = 512
    return pl.pallas_call(
        _route_kernel,
        grid=(T // tr,),
        in_specs=[pl.BlockSpec((E, tr), lambda i: (0, i)),
                  pl.BlockSpec((E, 1), lambda i: (0, 0))],
        out_specs=[pl.BlockSpec((TOP_K, tr), lambda i: (0, i)),
                   pl.BlockSpec((TOP_K, tr), lambda i: (0, i)),
                   pl.BlockSpec((tr, TOP_K), lambda i: (i, 0)),
                   pl.BlockSpec((E, 1), lambda i: (0, 0))],
        out_shape=[jax.ShapeDtypeStruct((TOP_K, T), jnp.int32),
                   jax.ShapeDtypeStruct((TOP_K, T), jnp.int32),
                   jax.ShapeDtypeStruct((T, TOP_K), F32),
                   jax.ShapeDtypeStruct((E, 1), jnp.int32)],
        scratch_shapes=[pltpu.VMEM((E, 1), F32)],
        compiler_params=_params(("arbitrary",), 32),
        name="router",
    )(logits_t, router_bias.reshape(E, 1))


def _slot_kernel(offs_ref, eid_ref, rank_ref, pos_ref):
    eid = eid_ref[...]
    pos = rank_ref[...]
    for e in range(N_EXPERTS):
        pos = pos + jnp.where(eid == e, offs_ref[e], 0)
    pos_ref[...] = pos


def _slots(offs, eid, rank):
    return pl.pallas_call(
        _slot_kernel,
        in_specs=[pl.BlockSpec(memory_space=pltpu.SMEM),
                  pl.BlockSpec(memory_space=pltpu.VMEM),
                  pl.BlockSpec(memory_space=pltpu.VMEM)],
        out_specs=pl.BlockSpec(memory_space=pltpu.VMEM),
        out_shape=jax.ShapeDtypeStruct(eid.shape, jnp.int32),
        name="slot_rows",
    )(offs, eid, rank)


def _dispatch_kernel(tt, pos_ref, h_ref, xs_ref, sem):
    def body(t, _):
        for k in range(TOP_K):
            pltpu.make_async_copy(h_ref.at[pl.ds(t, 1)], xs_ref.at[pl.ds(pos_ref[k, t], 1)], sem).start()
        return 0
    lax.fori_loop(0, tt, body, 0)
    for k in range(TOP_K):
        pltpu.make_async_copy(h_ref, xs_ref.at[pl.ds(0, tt)], sem).wait()


def _dispatch(pos, h2p):
    T, W = h2p.shape
    tt = 512
    return pl.pallas_call(
        functools.partial(_dispatch_kernel, tt),
        grid=(T // tt,),
        in_specs=[pl.BlockSpec((TOP_K, tt), lambda i: (0, i), memory_space=pltpu.SMEM),
                  pl.BlockSpec((tt, W), lambda i: (i, 0))],
        out_specs=pl.BlockSpec(memory_space=pl.ANY),
        out_shape=jax.ShapeDtypeStruct((T * TOP_K, W), U32),
        scratch_shapes=[pltpu.SemaphoreType.DMA],
        compiler_params=_params(("arbitrary",), 32),
        name="dispatch",
    )(pos, h2p)


EXPERT_TILE = 256


def _work_items(counts, n_rows):
    tm = EXPERT_TILE
    n_tiles = n_rows // tm
    ends = jnp.cumsum(counts)
    offs = ends - counts
    cuts = jnp.sort(jnp.concatenate([jnp.arange(n_tiles, dtype=jnp.int32) * tm, offs]))
    nxt = jnp.concatenate([cuts[1:], jnp.array([n_rows], jnp.int32)])
    tile = jnp.minimum(cuts // tm, n_tiles - 1)
    expert = jnp.minimum(jnp.sum((ends[None, :] <= cuts[:, None]).astype(jnp.int32), axis=1), N_EXPERTS - 1)
    lo = cuts - tile * tm
    hi = nxt - tile * tm
    fresh = jnp.concatenate([jnp.ones((1,), jnp.int32), (expert[1:] != expert[:-1]).astype(jnp.int32)])
    return offs, tile, expert, lo, hi, fresh


def _experts_kernel(tile_ref, exp_ref, lo_ref, hi_ref, fresh_ref,
                    xs_ref, wg_ref, wu_ref, wd_ref, ys_ref, wgb_ref, wub_ref, wdb_ref, acc_ref):
    i = pl.program_id(0)
    lo = lo_ref[i]
    hi = hi_ref[i]

    @pl.when(fresh_ref[i] == 1)
    def _():
        wgb_ref[...] = wg_ref[...].astype(BF16)
        wub_ref[...] = wu_ref[...].astype(BF16)
        wdb_ref[...] = wd_ref[...].astype(BF16)

    @pl.when(lo == 0)
    def _():
        acc_ref[...] = jnp.zeros_like(acc_ref)

    @pl.when(hi > lo)
    def _():
        x = _unpack_rows_bf16(xs_ref[...])
        a = jnp.dot(x, wgb_ref[...], preferred_element_type=F32)
        u = jnp.dot(x, wub_ref[...], preferred_element_type=F32)
        row = lax.broadcasted_iota(jnp.int32, (a.shape[0], 1), 0)
        h = jnp.where((row >= lo) & (row < hi), (a * _sigmoid(a)) * u, 0.0)
        acc_ref[...] += jnp.dot(h.astype(BF16), wdb_ref[...], preferred_element_type=F32)

    ys_ref[...] = _pack_halves(acc_ref[...])


def _experts(items, xs, wg, wu, wd):
    n_rows, W = xs.shape
    _, D, F = wg.shape
    tm = EXPERT_TILE
    n_items = items[0].shape[0]
    grid_spec = pltpu.PrefetchScalarGridSpec(
        num_scalar_prefetch=5,
        grid=(n_items,),
        in_specs=[pl.BlockSpec((tm, W), lambda i, tile, exp, lo, hi, fr: (tile[i], 0)),
                  pl.BlockSpec((None, D, F), lambda i, tile, exp, lo, hi, fr: (exp[i], 0, 0)),
                  pl.BlockSpec((None, D, F), lambda i, tile, exp, lo, hi, fr: (exp[i], 0, 0)),
                  pl.BlockSpec((None, F, D), lambda i, tile, exp, lo, hi, fr: (exp[i], 0, 0))],
        out_specs=pl.BlockSpec((tm, W), lambda i, tile, exp, lo, hi, fr: (tile[i], 0)),
        scratch_shapes=[pltpu.VMEM((D, F), BF16), pltpu.VMEM((D, F), BF16), pltpu.VMEM((F, D), BF16),
                        pltpu.VMEM((tm, D), F32)])
    return pl.pallas_call(
        _experts_kernel,
        grid_spec=grid_spec,
        out_shape=jax.ShapeDtypeStruct((n_rows, W), U32),
        compiler_params=_params(("arbitrary",), 56),
        name="routed_experts",
    )(*items, xs, wg, wu, wd)


def _combine_kernel(tt, pos_ref, w_ref, hp_ref, sg_ref, su_ref, sd_ref, x1_ref, gate2_ref, gf_ref,
                    ys_ref, y_ref, buf_ref, sem):
    def body(t, _):
        for k in range(TOP_K):
            pltpu.make_async_copy(ys_ref.at[pl.ds(pos_ref[k, t], 1)], buf_ref.at[k, pl.ds(t, 1)], sem).start()
        return 0
    lax.fori_loop(0, tt, body, 0)

    x = _unpack_rows_bf16(hp_ref[...])
    a = jnp.dot(x, sg_ref[...], preferred_element_type=F32)
    u = jnp.dot(x, su_ref[...], preferred_element_type=F32)
    moe = jnp.dot(((a * _sigmoid(a)) * u).astype(BF16), sd_ref[...], preferred_element_type=F32)

    for k in range(TOP_K):
        pltpu.make_async_copy(ys_ref.at[pl.ds(0, tt)], buf_ref.at[k], sem).wait()
    half = moe.shape[1] // 2
    w = w_ref[...]
    lo_acc = moe[:, :half]
    hi_acc = moe[:, half:]
    for k in range(TOP_K):
        lo, hi = _unpack_halves(buf_ref[k])
        wk = w[:, k:k + 1]
        lo_acc = lo_acc + wk * lo
        hi_acc = hi_acc + wk * hi
    moe = jnp.concatenate([lo_acc, hi_acc], axis=1)
    x2 = x1_ref[...] + gate2_ref[...] * moe
    xf = x2 * lax.rsqrt(jnp.mean(x2 * x2, axis=-1, keepdims=True) + EPS)
    y_ref[...] = xf * gf_ref[...]


def _combine(pos, w, h2p, ys, sg, su, sd, x1, mod3, gf, seq):
    T, D = x1.shape
    W = h2p.shape[1]
    tt = 256
    per_seq = seq // tt

    def const(shape):
        return pl.BlockSpec(shape, lambda i: (0,) * len(shape))

    return pl.pallas_call(
        functools.partial(_combine_kernel, tt),
        grid=(T // tt,),
        in_specs=[pl.BlockSpec((TOP_K, tt), lambda i: (0, i), memory_space=pltpu.SMEM),
                  pl.BlockSpec((tt, TOP_K), lambda i: (i, 0)),
                  pl.BlockSpec((tt, W), lambda i: (i, 0)),
                  const(sg.shape), const(su.shape), const(sd.shape),
                  pl.BlockSpec((tt, D), lambda i: (i, 0)),
                  pl.BlockSpec((None, 1, D), lambda i: ((i // per_seq) * 6 + 5, 0, 0)),
                  const((1, D)),
                  pl.BlockSpec(memory_space=pl.ANY)],
        out_specs=pl.BlockSpec((tt, D), lambda i: (i, 0)),
        out_shape=jax.ShapeDtypeStruct((T, D), F32),
        scratch_shapes=[pltpu.VMEM((TOP_K, tt, W), U32), pltpu.SemaphoreType.DMA],
        compiler_params=_params(("arbitrary",), 48),
        name="combine",
    )(pos, w, h2p, sg, su, sd, x1, mod3, gf, ys)


def kernel(x, c, positions, norm_mix_g, w_ada, b_ada, w_in, conv_w, w_conv_out, w_attn_out,
           w_o, norm_ffn_g, w_router, router_bias, w_exp_gate, w_exp_up, w_exp_down,
           w_sh_gate, w_sh_up, w_sh_down, norm_final_g):
    B, S, D = x.shape
    T = B * S
    assert w_ada.shape[0] == 1, "the final norm is fused into the single layer's combine kernel"
    l = 0
    x2d = x.reshape(T, D)
    mod3 = _ada(c, w_ada[l], b_ada[l]).reshape(B * 6, 1, D)
    h1 = _norm1(x2d, norm_mix_g[l].reshape(1, D), mod3, S)
    proj = _inproj(h1, w_in[l])
    cos2, sins = _rope_tables(positions)
    attn_o = _attention(proj, cos2, sins, B, S)
    x1, h2p, logits_t = _mixer_out(
        proj, attn_o, x2d, conv_w[l], w_conv_out[l].astype(BF16), w_attn_out[l].astype(BF16),
        w_o[l].astype(BF16), mod3, norm_ffn_g[l].reshape(1, D), w_router[l].T, S)
    eid, rank, gate_w, counts = _route(logits_t, router_bias[l])
    offs, *items = _work_items(counts.reshape(N_EXPERTS), T * TOP_K)
    pos = _slots(offs, eid, rank)
    xs = _dispatch(pos, h2p)
    ys = _experts(items, xs, w_exp_gate[l], w_exp_up[l], w_exp_down[l])
    y = _combine(pos, gate_w, h2p, ys, w_sh_gate[l].astype(BF16), w_sh_up[l].astype(BF16),
                 w_sh_down[l].astype(BF16), x1, mod3, norm_final_g.reshape(1, D), S)
    return y.reshape(B, S, D)
```

```python
import functools

import jax
import jax.numpy as jnp
from jax import lax
from jax.experimental import pallas as pl
from jax.experimental.pallas import tpu as pltpu

D_MODEL = 2048
HEAD_DIM = 128
ATTN_DILATIONS = (1, 4, 16)
N_GROUPS = 3
HEADS_PER_GROUP = 4
ATTN_WIDTH = N_GROUPS * HEADS_PER_GROUP * HEAD_DIM
ATTN_OUT_WIDTH = HEADS_PER_GROUP * HEAD_DIM
ROPE_THETA = 10000.0
Q_BLOCK = 128
CONV_WIDTH = D_MODEL // 2
CONV_K = 3
IN_COLS = 3 * CONV_WIDTH + 3 * ATTN_WIDTH + 2 * D_MODEL
N_EXPERTS = 64
TOP_K = 8
N_EXPERT_GROUPS = 8
TOP_GROUPS = 4
D_EXPERT = D_MODEL // 4
ROUTED_SCALE = 2.5
EPS = 1e-6
NEG_INF = -1e30

F32 = jnp.float32
BF16 = jnp.bfloat16
U32 = jnp.uint32
MIB = 1024 * 1024

PROJ_BLOCK = 512
N_PROJ_BLOCKS = IN_COLS // PROJ_BLOCK
GATE_SRC_BLOCK = (3 * CONV_WIDTH + 3 * ATTN_WIDTH) // PROJ_BLOCK
N_GATE_BLOCKS = 2 * D_MODEL // PROJ_BLOCK
COL_GCONV = 0
COL_GATTN = D_MODEL
COL_B = 2 * D_MODEL
COL_C = COL_B + CONV_WIDTH
COL_V = COL_C + CONV_WIDTH
COL_Q = COL_V + CONV_WIDTH
COL_K = COL_Q + ATTN_WIDTH
COL_VA = COL_K + ATTN_WIDTH


def _params(semantics, vmem_mib):
    return pltpu.CompilerParams(dimension_semantics=semantics, vmem_limit_bytes=vmem_mib * MIB)


def _sigmoid(x):
    return 1.0 / (1.0 + jnp.exp(-x))


def _pack_halves(x):
    n = x.shape[1] // 2
    return pltpu.pack_elementwise([x[:, :n], x[:, n:]], packed_dtype=BF16)


def _unpack_halves(xp):
    lo = pltpu.unpack_elementwise(xp, index=0, packed_dtype=BF16, unpacked_dtype=F32)
    hi = pltpu.unpack_elementwise(xp, index=1, packed_dtype=BF16, unpacked_dtype=F32)
    return lo, hi


def _unpack_rows_bf16(xp):
    lo, hi = _unpack_halves(xp)
    return jnp.concatenate([lo.astype(BF16), hi.astype(BF16)], axis=1)


ROW_TILE = 8
LANES = 128


def _store_token_tiles(ref, packed):
    n = packed.shape[0]
    for c in range(ROW_TILE):
        ref[pl.ds(c, n, stride=ROW_TILE), :] = packed[:, c * LANES:(c + 1) * LANES]


def _load_token_tiles(ref, n):
    return jnp.concatenate([ref[pl.ds(c, n, stride=ROW_TILE), :] for c in range(ROW_TILE)], axis=1)


def _ada_kernel(c_ref, w_ref, b_ref, o_ref):
    c = c_ref[...]
    s = c * _sigmoid(c)
    o_ref[...] = jnp.dot(s, w_ref[...], precision=lax.Precision.HIGHEST,
                         preferred_element_type=F32) + b_ref[...]


def _ada(c, w_ada, b_ada):
    B, D = c.shape
    N = w_ada.shape[1]
    tn = 1024
    return pl.pallas_call(
        _ada_kernel,
        grid=(N // tn,),
        in_specs=[pl.BlockSpec((B, D), lambda j: (0, 0)),
                  pl.BlockSpec((D, tn), lambda j: (0, j)),
                  pl.BlockSpec((1, tn), lambda j: (0, j))],
        out_specs=pl.BlockSpec((B, tn), lambda j: (0, j)),
        out_shape=jax.ShapeDtypeStruct((B, N), F32),
        compiler_params=_params(("arbitrary",), 40),
        name="ada_mod",
    )(c, w_ada, b_ada.reshape(1, N))


def _modnorm(x, g, scale, shift):
    xf = x * lax.rsqrt(jnp.mean(x * x, axis=-1, keepdims=True) + EPS)
    return (xf * g) * (1.0 + scale) + shift


def _norm1_kernel(x_ref, g_ref, sc_ref, sh_ref, o_ref):
    o_ref[...] = _modnorm(x_ref[...], g_ref[...], sc_ref[...], sh_ref[...]).astype(o_ref.dtype)


def _norm1(x2d, g, mod3, seq):
    T, D = x2d.shape
    tm = 512
    per_seq = seq // tm
    return pl.pallas_call(
        _norm1_kernel,
        grid=(T // tm,),
        in_specs=[pl.BlockSpec((tm, D), lambda i: (i, 0)),
                  pl.BlockSpec((1, D), lambda i: (0, 0)),
                  pl.BlockSpec((None, 1, D), lambda i: ((i // per_seq) * 6 + 1, 0, 0)),
                  pl.BlockSpec((None, 1, D), lambda i: ((i // per_seq) * 6 + 0, 0, 0))],
        out_specs=pl.BlockSpec((tm, D), lambda i: (i, 0)),
        out_shape=jax.ShapeDtypeStruct((T, D), BF16),
        compiler_params=_params(("arbitrary",), 32),
        name="norm_mix",
    )(x2d, g, mod3, mod3)


def _inproj_kernel(h_ref, w_ref, o_ref, wbf_ref):
    @pl.when(pl.program_id(1) == 0)
    def _():
        wbf_ref[...] = w_ref[...].astype(BF16)

    o_ref[...] = jnp.dot(h_ref[...], wbf_ref[...], preferred_element_type=F32).astype(o_ref.dtype)


def _proj_dst_block(j):
    return jnp.where(j < GATE_SRC_BLOCK, j + N_GATE_BLOCKS, j - GATE_SRC_BLOCK)


def _inproj(h, w_in):
    T, D = h.shape
    tm = 2048
    tn = PROJ_BLOCK
    return pl.pallas_call(
        _inproj_kernel,
        grid=(N_PROJ_BLOCKS, T // tm),
        in_specs=[pl.BlockSpec((tm, D), lambda j, i: (i, 0)),
                  pl.BlockSpec((D, tn), lambda j, i: (0, j))],
        out_specs=pl.BlockSpec((tm, tn), lambda j, i: (i, _proj_dst_block(j))),
        out_shape=jax.ShapeDtypeStruct((T, IN_COLS), BF16),
        scratch_shapes=[pltpu.VMEM((D, tn), BF16)],
        compiler_params=_params(("arbitrary", "arbitrary"), 48),
        name="in_proj",
    )(h, w_in)


def _rope_kernel(pos_ref, inv_ref, sign_ref, cos_ref, sin_ref):
    ang = pos_ref[...].astype(F32) * inv_ref[...]
    cos_ref[...] = jnp.cos(ang)
    sin_ref[...] = jnp.sin(ang) * sign_ref[...]


def _rope_tables(positions):
    T = positions.size
    ts = 1024
    inv = ROPE_THETA ** (-jnp.arange(0, HEAD_DIM, 2, dtype=F32) / HEAD_DIM)
    inv2 = jnp.concatenate([inv, inv]).reshape(1, HEAD_DIM)
    sign = jnp.concatenate([-jnp.ones((HEAD_DIM // 2,), F32), jnp.ones((HEAD_DIM // 2,), F32)]).reshape(1, HEAD_DIM)
    return pl.pallas_call(
        _rope_kernel,
        grid=(T // ts,),
        in_specs=[pl.BlockSpec((ts, 1), lambda i: (i, 0)),
                  pl.BlockSpec((1, HEAD_DIM), lambda i: (0, 0)),
                  pl.BlockSpec((1, HEAD_DIM), lambda i: (0, 0))],
        out_specs=[pl.BlockSpec((ts, HEAD_DIM), lambda i: (i, 0)),
                   pl.BlockSpec((ts, HEAD_DIM), lambda i: (i, 0))],
        out_shape=[jax.ShapeDtypeStruct((T, HEAD_DIM), F32)] * 2,
        compiler_params=_params(("arbitrary",), 32),
        name="rope_tables",
    )(positions.reshape(T, 1), inv2, sign)


ROW_CHUNK = 512


def _attn_group(r, seq, q_ref, k_ref, v_ref, cos_ref, sin_ref, nat_ref, qd_ref, kd_ref, vd_ref,
                acc_ref, m_ref, s_ref):
    n_sub = seq // r
    nb = n_sub // Q_BLOCK
    cls = n_sub + Q_BLOCK
    n_chunks = seq // ROW_CHUNK
    scale = HEAD_DIM ** -0.5

    def rope(t_ref, c0, mult):
        t = t_ref[pl.ds(c0, ROW_CHUNK), :].astype(F32)
        cs = cos_ref[pl.ds(c0, ROW_CHUNK), :]
        sn = sin_ref[pl.ds(c0, ROW_CHUNK), :]
        out = t * cs + pltpu.roll(t, HEAD_DIM // 2, 1) * sn
        return out * mult if mult is not None else out

    def deinterleave(dst_ref, dst_stride, dst_off):
        for rho in range(r):
            dst_ref[pl.ds(rho * dst_stride + dst_off, n_sub), :] = (
                nat_ref[pl.ds(rho, n_sub, stride=r), :].astype(BF16))

    zeros_blk = jnp.zeros((Q_BLOCK, HEAD_DIM), BF16)
    for rho in range(r):
        kd_ref[pl.ds(rho * cls, Q_BLOCK), :] = zeros_blk
        vd_ref[pl.ds(rho * cls, Q_BLOCK), :] = zeros_blk

    if r == 1:
        def fill(c, _):
            c0 = pl.multiple_of(c * ROW_CHUNK, ROW_CHUNK)
            qd_ref[pl.ds(c0, ROW_CHUNK), :] = rope(q_ref, c0, scale).astype(BF16)
            kd_ref[pl.ds(Q_BLOCK + c0, ROW_CHUNK), :] = rope(k_ref, c0, None).astype(BF16)
            vd_ref[pl.ds(Q_BLOCK + c0, ROW_CHUNK), :] = v_ref[pl.ds(c0, ROW_CHUNK), :]
            return 0
        lax.fori_loop(0, n_chunks, fill, 0)
    else:
        def fill_q(c, _):
            c0 = pl.multiple_of(c * ROW_CHUNK, ROW_CHUNK)
            nat_ref[pl.ds(c0, ROW_CHUNK), :] = rope(q_ref, c0, scale)
            return 0
        lax.fori_loop(0, n_chunks, fill_q, 0)
        deinterleave(qd_ref, n_sub, 0)

        def fill_k(c, _):
            c0 = pl.multiple_of(c * ROW_CHUNK, ROW_CHUNK)
            nat_ref[pl.ds(c0, ROW_CHUNK), :] = rope(k_ref, c0, None)
            return 0
        lax.fori_loop(0, n_chunks, fill_k, 0)
        deinterleave(kd_ref, cls, Q_BLOCK)

        def fill_v(c, _):
            c0 = pl.multiple_of(c * ROW_CHUNK, ROW_CHUNK)
            nat_ref[pl.ds(c0, ROW_CHUNK), :] = v_ref[pl.ds(c0, ROW_CHUNK), :].astype(F32)
            return 0
        lax.fori_loop(0, n_chunks, fill_v, 0)
        deinterleave(vd_ref, cls, Q_BLOCK)

    qi = lax.broadcasted_iota(jnp.int32, (Q_BLOCK, 2 * Q_BLOCK), 0)
    kj = lax.broadcasted_iota(jnp.int32, (Q_BLOCK, 2 * Q_BLOCK), 1)
    cur_ok = (kj >= Q_BLOCK) & (kj - Q_BLOCK <= qi)
    prev_ok = (kj < Q_BLOCK) & (kj >= qi)

    def block(blk, _):
        rho = blk // nb
        n = blk % nb
        qrow = pl.multiple_of(rho * n_sub + n * Q_BLOCK, Q_BLOCK)
        krow = pl.multiple_of(rho * cls + n * Q_BLOCK, Q_BLOCK)
        q = qd_ref[pl.ds(qrow, Q_BLOCK), :]
        kw = kd_ref[pl.ds(krow, 2 * Q_BLOCK), :]
        vw = vd_ref[pl.ds(krow, 2 * Q_BLOCK), :]
        s = lax.dot_general(q, kw, (((1,), (1,)), ((), ())), preferred_element_type=F32)
        ok = cur_ok | (prev_ok & (n > 0))
        s = jnp.where(ok, s, NEG_INF)
        m = jnp.max(s, axis=1, keepdims=True)
        p = jnp.exp(s - m)
        ssum = jnp.sum(p, axis=1, keepdims=True)
        acc = jnp.dot(p.astype(BF16), vw, preferred_element_type=F32)
        start = rho + n * (Q_BLOCK * r)
        if r == 1:
            rows = pl.ds(pl.multiple_of(start, Q_BLOCK), Q_BLOCK)
        else:
            rows = pl.ds(start, Q_BLOCK, stride=r)
        acc_ref[rows, :] = acc
        m_ref[rows, :] = jnp.broadcast_to(m, (Q_BLOCK, HEAD_DIM))
        s_ref[rows, :] = jnp.broadcast_to(ssum, (Q_BLOCK, HEAD_DIM))
        return 0

    lax.fori_loop(0, r * nb, block, 0, unroll=8)


def _attn_kernel(seq, q_ref, k_ref, v_ref, cos_ref, sin_ref, o_ref,
                 nat_ref, qd_ref, kd_ref, vd_ref, num_ref, den_ref, max_ref, acc_ref, m_ref, s_ref):
    g = pl.program_id(2)
    n_chunks = seq // ROW_CHUNK
    common = (q_ref, k_ref, v_ref, cos_ref, sin_ref, nat_ref, qd_ref, kd_ref, vd_ref)

    @pl.when(g == 0)
    def _():
        _attn_group(ATTN_DILATIONS[0], seq, *common, num_ref, max_ref, den_ref)

    def merge(last):
        def body(c, _):
            rows = pl.ds(pl.multiple_of(c * ROW_CHUNK, ROW_CHUNK), ROW_CHUNK)
            m_old = max_ref[rows, :]
            m_grp = m_ref[rows, :]
            m_new = jnp.maximum(m_old, m_grp)
            a = jnp.exp(m_old - m_new)
            b = jnp.exp(m_grp - m_new)
            num = num_ref[rows, :] * a + acc_ref[rows, :] * b
            den = den_ref[rows, :] * a + s_ref[rows, :] * b
            if last:
                o_ref[rows, :] = (num / den).astype(o_ref.dtype)
            else:
                num_ref[rows, :] = num
                den_ref[rows, :] = den
                max_ref[rows, :] = m_new
            return 0
        lax.fori_loop(0, n_chunks, body, 0)

    for gi in (1, 2):
        @pl.when(g == gi)
        def _(gi=gi):
            _attn_group(ATTN_DILATIONS[gi], seq, *common, acc_ref, m_ref, s_ref)
            merge(last=(gi == N_GROUPS - 1))


def _attention(proj, cos2, sins, batch, seq):
    T = proj.shape[0]
    qb, kb, vb = COL_Q // HEAD_DIM, COL_K // HEAD_DIM, COL_VA // HEAD_DIM
    pad_rows = seq + max(ATTN_DILATIONS) * Q_BLOCK

    def head_spec(base):
        return pl.BlockSpec((seq, HEAD_DIM), lambda b, j, g: (b, base + g * HEADS_PER_GROUP + j))

    tab_spec = pl.BlockSpec((seq, HEAD_DIM), lambda b, j, g: (b, 0))
    big = pltpu.VMEM((seq, HEAD_DIM), F32)
    return pl.pallas_call(
        functools.partial(_attn_kernel, seq),
        grid=(batch, HEADS_PER_GROUP, N_GROUPS),
        in_specs=[head_spec(qb), head_spec(kb), head_spec(vb), tab_spec, tab_spec],
        out_specs=pl.BlockSpec((seq, HEAD_DIM), lambda b, j, g: (b, j)),
        out_shape=jax.ShapeDtypeStruct((T, ATTN_OUT_WIDTH), BF16),
        scratch_shapes=[big,
                        pltpu.VMEM((seq, HEAD_DIM), BF16),
                        pltpu.VMEM((pad_rows, HEAD_DIM), BF16),
                        pltpu.VMEM((pad_rows, HEAD_DIM), BF16),
                        big, big, big, big, big, big],
        compiler_params=_params(("arbitrary", "arbitrary", "arbitrary"), 48),
        name="dilated_attn",
    )(proj, proj, proj, cos2, sins)


def _mix_kernel(seq, ts,
                b_ref, c_ref, v_ref, ch_ref, vh_ref, gc_ref, ga_ref, ao_ref, x_ref,
                cw_ref, wc_ref, wa_ref, wo_ref, gate1_ref, g2_ref, sc2_ref, sh2_ref, wr_ref,
                x1_ref, h2_ref, lt_ref):
    i = pl.program_id(0)
    not_start = ((i * ts) % seq != 0).astype(F32)
    u = c_ref[...].astype(F32) * v_ref[...].astype(F32)
    halo = ch_ref[...].astype(F32) * vh_ref[...].astype(F32) * not_start
    h1 = halo[15:16, :]
    h2 = halo[14:15, :]
    row = lax.broadcasted_iota(jnp.int32, u.shape, 0)
    up1 = jnp.where(row == 0, h1, pltpu.roll(u, 1, 0))
    up2 = jnp.where(row == 0, h2, jnp.where(row == 1, h1, pltpu.roll(u, 2, 0)))
    cw = cw_ref[...]
    conv = cw[0:1, :] * up2 + cw[1:2, :] * up1 + cw[2:3, :] * u
    yc = jnp.dot((b_ref[...].astype(F32) * conv).astype(BF16), wc_ref[...], preferred_element_type=F32)
    ya = jnp.dot(ao_ref[...], wa_ref[...], preferred_element_type=F32)
    merged = _sigmoid(gc_ref[...].astype(F32)) * yc + _sigmoid(ga_ref[...].astype(F32)) * ya
    mo = jnp.dot(merged.astype(BF16), wo_ref[...], preferred_element_type=F32)
    x1 = x_ref[...] + gate1_ref[...] * mo
    x1_ref[...] = x1
    h2n = _modnorm(x1, g2_ref[...], sc2_ref[...], sh2_ref[...])
    _store_token_tiles(h2_ref, _pack_halves(h2n))
    lt_ref[...] = lax.dot_general(wr_ref[...], h2n, (((1,), (1,)), ((), ())),
                                  precision=lax.Precision.HIGHEST, preferred_element_type=F32)


def _mixer_out(proj, attn_o, x2d, conv_w, wc, wa, wo, mod3, g2, wr_t, seq):
    T, D = x2d.shape
    ts = 256
    per_seq = seq // ts
    halo_rows = 16

    def col(width, off):
        return pl.BlockSpec((ts, width), lambda i: (i, off // width))

    def halo(off):
        return pl.BlockSpec((halo_rows, CONV_WIDTH),
                            lambda i: (jnp.maximum(i * (ts // halo_rows) - 1, 0), off // CONV_WIDTH))

    def const(shape):
        return pl.BlockSpec(shape, lambda i: (0,) * len(shape), pipeline_mode=pl.Buffered(1))

    def mod(k):
        return pl.BlockSpec((None, 1, D), lambda i: ((i // per_seq) * 6 + k, 0, 0))

    return pl.pallas_call(
        functools.partial(_mix_kernel, seq, ts),
        grid=(T // ts,),
        in_specs=[col(CONV_WIDTH, COL_B), col(CONV_WIDTH, COL_C), col(CONV_WIDTH, COL_V),
                  halo(COL_C), halo(COL_V),
                  col(D, COL_GCONV), col(D, COL_GATTN),
                  pl.BlockSpec((ts, ATTN_OUT_WIDTH), lambda i: (i, 0)),
                  pl.BlockSpec((ts, D), lambda i: (i, 0)),
                  const((CONV_K, CONV_WIDTH)), const((CONV_WIDTH, D)), const((ATTN_OUT_WIDTH, D)),
                  const((D, D)),
                  mod(2), const((1, D)), mod(4), mod(3), const((N_EXPERTS, D))],
        out_specs=[pl.BlockSpec((ts, D), lambda i: (i, 0)),
                   pl.BlockSpec((ts * ROW_TILE, LANES), lambda i: (i, 0)),
                   pl.BlockSpec((N_EXPERTS, ts), lambda i: (0, i))],
        out_shape=[jax.ShapeDtypeStruct((T, D), F32),
                   jax.ShapeDtypeStruct((T * ROW_TILE, LANES), U32),
                   jax.ShapeDtypeStruct((N_EXPERTS, T), F32)],
        compiler_params=_params(("arbitrary",), 56),
        name="mixer_out",
    )(proj, proj, proj, proj, proj, proj, proj, attn_o, x2d,
      conv_w, wc, wa, wo, mod3, g2, mod3, mod3, wr_t)


def _route_kernel(lt_ref, bias_ref, eid_ref, rank_ref, w_ref, cnt_ref, carry_ref):
    @pl.when(pl.program_id(0) == 0)
    def _():
        carry_ref[...] = jnp.zeros_like(carry_ref)

    lt = lt_ref[...]
    tr = lt.shape[1]
    per = N_EXPERTS // N_EXPERT_GROUPS
    scores = _sigmoid(lt)
    sel = scores + bias_ref[...]
    sel3 = sel.reshape(N_EXPERT_GROUPS, per, tr)
    sub = lax.broadcasted_iota(jnp.int32, sel3.shape, 1)
    top1 = jnp.max(sel3, axis=1, keepdims=True)
    first = jnp.min(jnp.where(sel3 == top1, sub, per), axis=1, keepdims=True)
    top2 = jnp.max(jnp.where(sub == first, -jnp.inf, sel3), axis=1, keepdims=True)
    gs = (top1 + top2).reshape(N_EXPERT_GROUPS, tr)
    gidx = lax.broadcasted_iota(jnp.int32, gs.shape, 0)
    grank = jnp.zeros(gs.shape, jnp.int32)
    for other in range(N_EXPERT_GROUPS):
        row = gs[other:other + 1, :]
        grank += ((row > gs) | ((row == gs) & (other < gidx))).astype(jnp.int32)
    gmask = grank < TOP_GROUPS
    emask = jnp.broadcast_to(gmask[:, None, :], sel3.shape).reshape(N_EXPERTS, tr)
    cand = jnp.where(emask, sel, NEG_INF)
    eidx = lax.broadcasted_iota(jnp.int32, cand.shape, 0)
    hits, eids, ws = [], [], []
    for _ in range(TOP_K):
        mx = jnp.max(cand, axis=0, keepdims=True)
        idx = jnp.min(jnp.where(cand == mx, eidx, N_EXPERTS), axis=0, keepdims=True)
        hit = eidx == idx
        hits.append(hit)
        eids.append(idx)
        ws.append(jnp.sum(jnp.where(hit, scores, 0.0), axis=0, keepdims=True))
        cand = jnp.where(hit, -jnp.inf, cand)
    chosen = hits[0]
    for hit in hits[1:]:
        chosen = chosen | hit
    chosen = chosen.astype(F32)
    before = (lax.broadcasted_iota(jnp.int32, (tr, tr), 0) < lax.broadcasted_iota(jnp.int32, (tr, tr), 1))
    seen = jnp.dot(chosen.astype(BF16), before.astype(BF16), preferred_element_type=F32) + carry_ref[...]
    ranks = [jnp.sum(jnp.where(hit, seen, 0.0), axis=0, keepdims=True) for hit in hits]
    carry_ref[...] += jnp.sum(chosen, axis=1, keepdims=True)
    cnt_ref[...] = carry_ref[...].astype(jnp.int32)
    wsum = ws[0]
    for w in ws[1:]:
        wsum = wsum + w
    w8 = jnp.concatenate(ws, axis=0) / wsum * ROUTED_SCALE
    w_ref[...] = w8.T
    eid_ref[...] = jnp.concatenate(eids, axis=0)
    rank_ref[...] = jnp.concatenate(ranks, axis=0).astype(jnp.int32)


def _route(logits_t, router_bias):
    E, T = logits_t.shape
    tr = 512
    return pl.pallas_call(
        _route_kernel,
        grid=(T // tr,),
        in_specs=[pl.BlockSpec((E, tr), lambda i: (0, i)),
                  pl.BlockSpec((E, 1), lambda i: (0, 0))],
        out_specs=[pl.BlockSpec((TOP_K, tr), lambda i: (0, i)),
                   pl.BlockSpec((TOP_K, tr), lambda i: (0, i)),
                   pl.BlockSpec((tr, TOP_K), lambda i: (i, 0)),
                   pl.BlockSpec((E, 1), lambda i: (0, 0))],
        out_shape=[jax.ShapeDtypeStruct((TOP_K, T), jnp.int32),
                   jax.ShapeDtypeStruct((TOP_K, T), jnp.int32),
                   jax.ShapeDtypeStruct((T, TOP_K), F32),
                   jax.ShapeDtypeStruct((E, 1), jnp.int32)],
        scratch_shapes=[pltpu.VMEM((E, 1), F32)],
        compiler_params=_params(("arbitrary",), 32),
        name="router",
    )(logits_t, router_bias.reshape(E, 1))


def _slot_kernel(offs_ref, eid_ref, rank_ref, pos_ref):
    eid = eid_ref[...]
    pos = rank_ref[...]
    for e in range(N_EXPERTS):
        pos = pos + jnp.where(eid == e, offs_ref[e], 0)
    pos_ref[...] = pos * ROW_TILE


def _slots(offs, eid, rank):
    return pl.pallas_call(
        _slot_kernel,
        in_specs=[pl.BlockSpec(memory_space=pltpu.SMEM),
                  pl.BlockSpec(memory_space=pltpu.VMEM),
                  pl.BlockSpec(memory_space=pltpu.VMEM)],
        out_specs=pl.BlockSpec(memory_space=pltpu.VMEM),
        out_shape=jax.ShapeDtypeStruct(eid.shape, jnp.int32),
        name="slot_rows",
    )(offs, eid, rank)


def _tile_at(ref, row):
    return ref.at[pl.ds(pl.multiple_of(row, ROW_TILE), ROW_TILE)]


def _dispatch_kernel(tt, pos_ref, h_ref, xs_ref, sem):
    def body(t, _):
        src = _tile_at(h_ref, t * ROW_TILE)
        for k in range(TOP_K):
            pltpu.make_async_copy(src, _tile_at(xs_ref, pos_ref[t * TOP_K + k]), sem).start()
        return 0
    lax.fori_loop(0, tt, body, 0)
    for k in range(TOP_K):
        pltpu.make_async_copy(h_ref, xs_ref.at[pl.ds(0, tt * ROW_TILE)], sem).wait()


def _dispatch(pos, h2p):
    T = h2p.shape[0] // ROW_TILE
    tt = 512
    return pl.pallas_call(
        functools.partial(_dispatch_kernel, tt),
        grid=(T // tt,),
        in_specs=[pl.BlockSpec((tt * TOP_K,), lambda i: (i,), memory_space=pltpu.SMEM),
                  pl.BlockSpec((tt * ROW_TILE, LANES), lambda i: (i, 0))],
        out_specs=pl.BlockSpec(memory_space=pl.ANY),
        out_shape=jax.ShapeDtypeStruct((T * TOP_K * ROW_TILE, LANES), U32),
        scratch_shapes=[pltpu.SemaphoreType.DMA],
        compiler_params=_params(("arbitrary",), 32),
        name="dispatch",
    )(pos, h2p)


EXPERT_TILE = 256


def _work_items(counts, n_rows):
    tm = EXPERT_TILE
    n_tiles = n_rows // tm
    ends = jnp.cumsum(counts)
    offs = ends - counts
    cuts = jnp.sort(jnp.concatenate([jnp.arange(n_tiles, dtype=jnp.int32) * tm, offs]))
    nxt = jnp.concatenate([cuts[1:], jnp.array([n_rows], jnp.int32)])
    tile = jnp.minimum(cuts // tm, n_tiles - 1)
    expert = jnp.minimum(jnp.sum((ends[None, :] <= cuts[:, None]).astype(jnp.int32), axis=1), N_EXPERTS - 1)
    lo = cuts - tile * tm
    hi = nxt - tile * tm
    fresh = jnp.concatenate([jnp.ones((1,), jnp.int32), (expert[1:] != expert[:-1]).astype(jnp.int32)])
    return offs, tile, expert, lo, hi, fresh


def _experts_kernel(tile_ref, exp_ref, lo_ref, hi_ref, fresh_ref,
                    xs_ref, wg_ref, wu_ref, wd_ref, ys_ref, wgb_ref, wub_ref, wdb_ref, acc_ref):
    i = pl.program_id(0)
    lo = lo_ref[i]
    hi = hi_ref[i]

    @pl.when(fresh_ref[i] == 1)
    def _():
        wgb_ref[...] = wg_ref[...].astype(BF16)
        wub_ref[...] = wu_ref[...].astype(BF16)
        wdb_ref[...] = wd_ref[...].astype(BF16)

    @pl.when(lo == 0)
    def _():
        acc_ref[...] = jnp.zeros_like(acc_ref)

    @pl.when(hi > lo)
    def _():
        x = _unpack_rows_bf16(_load_token_tiles(xs_ref, acc_ref.shape[0]))
        a = jnp.dot(x, wgb_ref[...], preferred_element_type=F32)
        u = jnp.dot(x, wub_ref[...], preferred_element_type=F32)
        row = lax.broadcasted_iota(jnp.int32, (a.shape[0], 1), 0)
        h = jnp.where((row >= lo) & (row < hi), (a * _sigmoid(a)) * u, 0.0)
        acc_ref[...] += jnp.dot(h.astype(BF16), wdb_ref[...], preferred_element_type=F32)

    _store_token_tiles(ys_ref, _pack_halves(acc_ref[...]))


def _experts(items, xs, wg, wu, wd):
    n_rows = xs.shape[0] // ROW_TILE
    _, D, F = wg.shape
    tm = EXPERT_TILE
    n_items = items[0].shape[0]
    tiles = pl.BlockSpec((tm * ROW_TILE, LANES), lambda i, tile, exp, lo, hi, fr: (tile[i], 0))
    grid_spec = pltpu.PrefetchScalarGridSpec(
        num_scalar_prefetch=5,
        grid=(n_items,),
        in_specs=[tiles,
                  pl.BlockSpec((None, D, F), lambda i, tile, exp, lo, hi, fr: (exp[i], 0, 0)),
                  pl.BlockSpec((None, D, F), lambda i, tile, exp, lo, hi, fr: (exp[i], 0, 0)),
                  pl.BlockSpec((None, F, D), lambda i, tile, exp, lo, hi, fr: (exp[i], 0, 0))],
        out_specs=tiles,
        scratch_shapes=[pltpu.VMEM((D, F), BF16), pltpu.VMEM((D, F), BF16), pltpu.VMEM((F, D), BF16),
                        pltpu.VMEM((tm, D), F32)])
    return pl.pallas_call(
        _experts_kernel,
        grid_spec=grid_spec,
        out_shape=jax.ShapeDtypeStruct((n_rows * ROW_TILE, LANES), U32),
        compiler_params=_params(("arbitrary",), 56),
        name="routed_experts",
    )(*items, xs, wg, wu, wd)


def _combine_kernel(tt, pos_ref, w_ref, hp_ref, sg_ref, su_ref, sd_ref, x1_ref, gate2_ref, gf_ref,
                    ys_ref, y_ref, buf_ref, sem):
    def body(t, _):
        for k in range(TOP_K):
            pltpu.make_async_copy(_tile_at(ys_ref, pos_ref[t * TOP_K + k]), _tile_at(buf_ref.at[k], t * ROW_TILE), sem).start()
        return 0
    lax.fori_loop(0, tt, body, 0)

    x = _unpack_rows_bf16(_load_token_tiles(hp_ref, tt))
    a = jnp.dot(x, sg_ref[...], preferred_element_type=F32)
    u = jnp.dot(x, su_ref[...], preferred_element_type=F32)
    moe = jnp.dot(((a * _sigmoid(a)) * u).astype(BF16), sd_ref[...], preferred_element_type=F32)

    for k in range(TOP_K):
        pltpu.make_async_copy(ys_ref.at[pl.ds(0, tt * ROW_TILE)], buf_ref.at[k], sem).wait()
    half = moe.shape[1] // 2
    w = w_ref[...]
    lo_acc = moe[:, :half]
    hi_acc = moe[:, half:]
    for k in range(TOP_K):
        lo, hi = _unpack_halves(_load_token_tiles(buf_ref.at[k], tt))
        wk = w[:, k:k + 1]
        lo_acc = lo_acc + wk * lo
        hi_acc = hi_acc + wk * hi
    moe = jnp.concatenate([lo_acc, hi_acc], axis=1)
    x2 = x1_ref[...] + gate2_ref[...] * moe
    xf = x2 * lax.rsqrt(jnp.mean(x2 * x2, axis=-1, keepdims=True) + EPS)
    y_ref[...] = xf * gf_ref[...]


def _combine(pos, w, h2p, ys, sg, su, sd, x1, mod3, gf, seq):
    T, D = x1.shape
    assert D // 2 == ROW_TILE * LANES, "a packed token row must fill exactly one (8, 128) tile"
    tt = 256
    per_seq = seq // tt

    def const(shape):
        return pl.BlockSpec(shape, lambda i: (0,) * len(shape))

    return pl.pallas_call(
        functools.partial(_combine_kernel, tt),
        grid=(T // tt,),
        in_specs=[pl.BlockSpec((tt * TOP_K,), lambda i: (i,), memory_space=pltpu.SMEM),
                  pl.BlockSpec((tt, TOP_K), lambda i: (i, 0)),
                  pl.BlockSpec((tt * ROW_TILE, LANES), lambda i: (i, 0)),
                  const(sg.shape), const(su.shape), const(sd.shape),
                  pl.BlockSpec((tt, D), lambda i: (i, 0)),
                  pl.BlockSpec((None, 1, D), lambda i: ((i // per_seq) * 6 + 5, 0, 0)),
                  const((1, D)),
                  pl.BlockSpec(memory_space=pl.ANY)],
        out_specs=pl.BlockSpec((tt, D), lambda i: (i, 0)),
        out_shape=jax.ShapeDtypeStruct((T, D), F32),
        scratch_shapes=[pltpu.VMEM((TOP_K, tt * ROW_TILE, LANES), U32), pltpu.SemaphoreType.DMA],
        compiler_params=_params(("arbitrary",), 48),
        name="combine",
    )(pos, w, h2p, sg, su, sd, x1, mod3, gf, ys)


def kernel(x, c, positions, norm_mix_g, w_ada, b_ada, w_in, conv_w, w_conv_out, w_attn_out,
           w_o, norm_ffn_g, w_router, router_bias, w_exp_gate, w_exp_up, w_exp_down,
           w_sh_gate, w_sh_up, w_sh_down, norm_final_g):
    B, S, D = x.shape
    T = B * S
    assert w_ada.shape[0] == 1, "the final norm is fused into the single layer's combine kernel"
    l = 0
    x2d = x.reshape(T, D)
    mod3 = _ada(c, w_ada[l], b_ada[l]).reshape(B * 6, 1, D)
    h1 = _norm1(x2d, norm_mix_g[l].reshape(1, D), mod3, S)
    proj = _inproj(h1, w_in[l])
    cos2, sins = _rope_tables(positions)
    attn_o = _attention(proj, cos2, sins, B, S)
    x1, h2p, logits_t = _mixer_out(
        proj, attn_o, x2d, conv_w[l], w_conv_out[l].astype(BF16), w_attn_out[l].astype(BF16),
        w_o[l].astype(BF16), mod3, norm_ffn_g[l].reshape(1, D), w_router[l].T, S)
    eid, rank, gate_w, counts = _route(logits_t, router_bias[l])
    offs, *items = _work_items(counts.reshape(N_EXPERTS), T * TOP_K)
    pos = _slots(offs, eid, rank).T.reshape(T * TOP_K)
    xs = _dispatch(pos, h2p)
    ys = _experts(items, xs, w_exp_gate[l], w_exp_up[l], w_exp_down[l])
    y = _combine(pos, gate_w, h2p, ys, w_sh_gate[l].astype(BF16), w_sh_up[l].astype(BF16),
                 w_sh_down[l].astype(BF16), x1, mod3, norm_final_g.reshape(1, D), S)
    return y.reshape(B, S, D)
```

```python
import functools

import jax
import jax.numpy as jnp
from jax import lax
from jax.experimental import pallas as pl
from jax.experimental.pallas import tpu as pltpu

D_MODEL = 2048
HEAD_DIM = 128
ATTN_DILATIONS = (1, 4, 16)
N_GROUPS = 3
HEADS_PER_GROUP = 4
ATTN_WIDTH = N_GROUPS * HEADS_PER_GROUP * HEAD_DIM
ATTN_OUT_WIDTH = HEADS_PER_GROUP * HEAD_DIM
ROPE_THETA = 10000.0
Q_BLOCK = 128
CONV_WIDTH = D_MODEL // 2
CONV_K = 3
IN_COLS = 3 * CONV_WIDTH + 3 * ATTN_WIDTH + 2 * D_MODEL
N_EXPERTS = 64
TOP_K = 8
N_EXPERT_GROUPS = 8
TOP_GROUPS = 4
D_EXPERT = D_MODEL // 4
ROUTED_SCALE = 2.5
EPS = 1e-6
NEG_INF = -1e30

F32 = jnp.float32
BF16 = jnp.bfloat16
U32 = jnp.uint32
MIB = 1024 * 1024

PROJ_BLOCK = 512
N_PROJ_BLOCKS = IN_COLS // PROJ_BLOCK
GATE_SRC_BLOCK = (3 * CONV_WIDTH + 3 * ATTN_WIDTH) // PROJ_BLOCK
N_GATE_BLOCKS = 2 * D_MODEL // PROJ_BLOCK
COL_GCONV = 0
COL_GATTN = D_MODEL
COL_B = 2 * D_MODEL
COL_C = COL_B + CONV_WIDTH
COL_V = COL_C + CONV_WIDTH
COL_Q = COL_V + CONV_WIDTH
COL_K = COL_Q + ATTN_WIDTH
COL_VA = COL_K + ATTN_WIDTH


def _params(semantics, vmem_mib):
    return pltpu.CompilerParams(dimension_semantics=semantics, vmem_limit_bytes=vmem_mib * MIB)


def _sigmoid(x):
    return 1.0 / (1.0 + jnp.exp(-x))


def _pack_halves(x):
    n = x.shape[1] // 2
    return pltpu.pack_elementwise([x[:, :n], x[:, n:]], packed_dtype=BF16)


def _unpack_halves(xp):
    lo = pltpu.unpack_elementwise(xp, index=0, packed_dtype=BF16, unpacked_dtype=F32)
    hi = pltpu.unpack_elementwise(xp, index=1, packed_dtype=BF16, unpacked_dtype=F32)
    return lo, hi


def _unpack_rows_bf16(xp):
    lo, hi = _unpack_halves(xp)
    return jnp.concatenate([lo.astype(BF16), hi.astype(BF16)], axis=1)


ROW_TILE = 8
LANES = 128


def _store_token_tiles(ref, packed):
    n = packed.shape[0]
    for c in range(ROW_TILE):
        ref[pl.ds(c, n, stride=ROW_TILE), :] = packed[:, c * LANES:(c + 1) * LANES]


def _load_token_tiles(ref, n):
    return jnp.concatenate([ref[pl.ds(c, n, stride=ROW_TILE), :] for c in range(ROW_TILE)], axis=1)


def _ada_kernel(c_ref, w_ref, b_ref, o_ref):
    c = c_ref[...]
    s = c * _sigmoid(c)
    o_ref[...] = jnp.dot(s, w_ref[...], precision=lax.Precision.HIGHEST,
                         preferred_element_type=F32) + b_ref[...]


def _ada(c, w_ada, b_ada):
    B, D = c.shape
    N = w_ada.shape[1]
    tn = 1024
    return pl.pallas_call(
        _ada_kernel,
        grid=(N // tn,),
        in_specs=[pl.BlockSpec((B, D), lambda j: (0, 0)),
                  pl.BlockSpec((D, tn), lambda j: (0, j)),
                  pl.BlockSpec((1, tn), lambda j: (0, j))],
        out_specs=pl.BlockSpec((B, tn), lambda j: (0, j)),
        out_shape=jax.ShapeDtypeStruct((B, N), F32),
        compiler_params=_params(("arbitrary",), 40),
        name="ada_mod",
    )(c, w_ada, b_ada.reshape(1, N))


def _modnorm(x, g, scale, shift):
    xf = x * lax.rsqrt(jnp.mean(x * x, axis=-1, keepdims=True) + EPS)
    return (xf * g) * (1.0 + scale) + shift


def _norm1_kernel(x_ref, g_ref, sc_ref, sh_ref, o_ref):
    o_ref[...] = _modnorm(x_ref[...], g_ref[...], sc_ref[...], sh_ref[...]).astype(o_ref.dtype)


def _norm1(x2d, g, mod3, seq):
    T, D = x2d.shape
    tm = 512
    per_seq = seq // tm
    return pl.pallas_call(
        _norm1_kernel,
        grid=(T // tm,),
        in_specs=[pl.BlockSpec((tm, D), lambda i: (i, 0)),
                  pl.BlockSpec((1, D), lambda i: (0, 0)),
                  pl.BlockSpec((None, 1, D), lambda i: ((i // per_seq) * 6 + 1, 0, 0)),
                  pl.BlockSpec((None, 1, D), lambda i: ((i // per_seq) * 6 + 0, 0, 0))],
        out_specs=pl.BlockSpec((tm, D), lambda i: (i, 0)),
        out_shape=jax.ShapeDtypeStruct((T, D), BF16),
        compiler_params=_params(("arbitrary",), 32),
        name="norm_mix",
    )(x2d, g, mod3, mod3)


def _inproj_kernel(h_ref, w_ref, o_ref, wbf_ref):
    @pl.when(pl.program_id(1) == 0)
    def _():
        wbf_ref[...] = w_ref[...].astype(BF16)

    o_ref[...] = jnp.dot(h_ref[...], wbf_ref[...], preferred_element_type=F32).astype(o_ref.dtype)


def _proj_dst_block(j):
    return jnp.where(j < GATE_SRC_BLOCK, j + N_GATE_BLOCKS, j - GATE_SRC_BLOCK)


def _inproj(h, w_in):
    T, D = h.shape
    tm = 2048
    tn = PROJ_BLOCK
    return pl.pallas_call(
        _inproj_kernel,
        grid=(N_PROJ_BLOCKS, T // tm),
        in_specs=[pl.BlockSpec((tm, D), lambda j, i: (i, 0)),
                  pl.BlockSpec((D, tn), lambda j, i: (0, j))],
        out_specs=pl.BlockSpec((tm, tn), lambda j, i: (i, _proj_dst_block(j))),
        out_shape=jax.ShapeDtypeStruct((T, IN_COLS), BF16),
        scratch_shapes=[pltpu.VMEM((D, tn), BF16)],
        compiler_params=_params(("arbitrary", "arbitrary"), 48),
        name="in_proj",
    )(h, w_in)


def _rope_kernel(pos_ref, inv_ref, sign_ref, cos_ref, sin_ref):
    ang = pos_ref[...].astype(F32) * inv_ref[...]
    cos_ref[...] = jnp.cos(ang)
    sin_ref[...] = jnp.sin(ang) * sign_ref[...]


def _rope_tables(positions):
    T = positions.size
    ts = 1024
    inv = ROPE_THETA ** (-jnp.arange(0, HEAD_DIM, 2, dtype=F32) / HEAD_DIM)
    inv2 = jnp.concatenate([inv, inv]).reshape(1, HEAD_DIM)
    sign = jnp.concatenate([-jnp.ones((HEAD_DIM // 2,), F32), jnp.ones((HEAD_DIM // 2,), F32)]).reshape(1, HEAD_DIM)
    return pl.pallas_call(
        _rope_kernel,
        grid=(T // ts,),
        in_specs=[pl.BlockSpec((ts, 1), lambda i: (i, 0)),
                  pl.BlockSpec((1, HEAD_DIM), lambda i: (0, 0)),
                  pl.BlockSpec((1, HEAD_DIM), lambda i: (0, 0))],
        out_specs=[pl.BlockSpec((ts, HEAD_DIM), lambda i: (i, 0)),
                   pl.BlockSpec((ts, HEAD_DIM), lambda i: (i, 0))],
        out_shape=[jax.ShapeDtypeStruct((T, HEAD_DIM), F32)] * 2,
        compiler_params=_params(("arbitrary",), 32),
        name="rope_tables",
    )(positions.reshape(T, 1), inv2, sign)


ROW_CHUNK = 512


def _attn_group(r, seq, q_ref, k_ref, v_ref, cos_ref, sin_ref, nat_ref, qd_ref, kd_ref, vd_ref,
                acc_ref, m_ref, s_ref):
    n_sub = seq // r
    nb = n_sub // Q_BLOCK
    cls = n_sub + Q_BLOCK
    n_chunks = seq // ROW_CHUNK
    scale = HEAD_DIM ** -0.5

    def rope(t_ref, c0, mult):
        t = t_ref[pl.ds(c0, ROW_CHUNK), :].astype(F32)
        cs = cos_ref[pl.ds(c0, ROW_CHUNK), :]
        sn = sin_ref[pl.ds(c0, ROW_CHUNK), :]
        out = t * cs + pltpu.roll(t, HEAD_DIM // 2, 1) * sn
        return out * mult if mult is not None else out

    def deinterleave(dst_ref, dst_stride, dst_off):
        for rho in range(r):
            dst_ref[pl.ds(rho * dst_stride + dst_off, n_sub), :] = (
                nat_ref[pl.ds(rho, n_sub, stride=r), :].astype(BF16))

    zeros_blk = jnp.zeros((Q_BLOCK, HEAD_DIM), BF16)
    for rho in range(r):
        kd_ref[pl.ds(rho * cls, Q_BLOCK), :] = zeros_blk
        vd_ref[pl.ds(rho * cls, Q_BLOCK), :] = zeros_blk

    if r == 1:
        def fill(c, _):
            c0 = pl.multiple_of(c * ROW_CHUNK, ROW_CHUNK)
            qd_ref[pl.ds(c0, ROW_CHUNK), :] = rope(q_ref, c0, scale).astype(BF16)
            kd_ref[pl.ds(Q_BLOCK + c0, ROW_CHUNK), :] = rope(k_ref, c0, None).astype(BF16)
            vd_ref[pl.ds(Q_BLOCK + c0, ROW_CHUNK), :] = v_ref[pl.ds(c0, ROW_CHUNK), :]
            return 0
        lax.fori_loop(0, n_chunks, fill, 0)
    else:
        def fill_q(c, _):
            c0 = pl.multiple_of(c * ROW_CHUNK, ROW_CHUNK)
            nat_ref[pl.ds(c0, ROW_CHUNK), :] = rope(q_ref, c0, scale)
            return 0
        lax.fori_loop(0, n_chunks, fill_q, 0)
        deinterleave(qd_ref, n_sub, 0)

        def fill_k(c, _):
            c0 = pl.multiple_of(c * ROW_CHUNK, ROW_CHUNK)
            nat_ref[pl.ds(c0, ROW_CHUNK), :] = rope(k_ref, c0, None)
            return 0
        lax.fori_loop(0, n_chunks, fill_k, 0)
        deinterleave(kd_ref, cls, Q_BLOCK)

        def fill_v(c, _):
            c0 = pl.multiple_of(c * ROW_CHUNK, ROW_CHUNK)
            nat_ref[pl.ds(c0, ROW_CHUNK), :] = v_ref[pl.ds(c0, ROW_CHUNK), :].astype(F32)
            return 0
        lax.fori_loop(0, n_chunks, fill_v, 0)
        deinterleave(vd_ref, cls, Q_BLOCK)

    qi = lax.broadcasted_iota(jnp.int32, (Q_BLOCK, 2 * Q_BLOCK), 0)
    kj = lax.broadcasted_iota(jnp.int32, (Q_BLOCK, 2 * Q_BLOCK), 1)
    cur_ok = (kj >= Q_BLOCK) & (kj - Q_BLOCK <= qi)
    prev_ok = (kj < Q_BLOCK) & (kj >= qi)

    def block(blk, _):
        rho = blk // nb
        n = blk % nb
        qrow = pl.multiple_of(rho * n_sub + n * Q_BLOCK, Q_BLOCK)
        krow = pl.multiple_of(rho * cls + n * Q_BLOCK, Q_BLOCK)
        q = qd_ref[pl.ds(qrow, Q_BLOCK), :]
        kw = kd_ref[pl.ds(krow, 2 * Q_BLOCK), :]
        vw = vd_ref[pl.ds(krow, 2 * Q_BLOCK), :]
        s = lax.dot_general(q, kw, (((1,), (1,)), ((), ())), preferred_element_type=F32)
        ok = cur_ok | (prev_ok & (n > 0))
        s = jnp.where(ok, s, NEG_INF)
        m = jnp.max(s, axis=1, keepdims=True)
        p = jnp.exp(s - m)
        ssum = jnp.sum(p, axis=1, keepdims=True)
        acc = jnp.dot(p.astype(BF16), vw, preferred_element_type=F32)
        start = rho + n * (Q_BLOCK * r)
        if r == 1:
            rows = pl.ds(pl.multiple_of(start, Q_BLOCK), Q_BLOCK)
        else:
            rows = pl.ds(start, Q_BLOCK, stride=r)
        acc_ref[rows, :] = acc
        m_ref[rows, :] = jnp.broadcast_to(m, (Q_BLOCK, HEAD_DIM))
        s_ref[rows, :] = jnp.broadcast_to(ssum, (Q_BLOCK, HEAD_DIM))
        return 0

    lax.fori_loop(0, r * nb, block, 0, unroll=8)


def _attn_kernel(seq, q_ref, k_ref, v_ref, cos_ref, sin_ref, o_ref,
                 nat_ref, qd_ref, kd_ref, vd_ref, num_ref, den_ref, max_ref, acc_ref, m_ref, s_ref):
    g = pl.program_id(2)
    n_chunks = seq // ROW_CHUNK
    common = (q_ref, k_ref, v_ref, cos_ref, sin_ref, nat_ref, qd_ref, kd_ref, vd_ref)

    @pl.when(g == 0)
    def _():
        _attn_group(ATTN_DILATIONS[0], seq, *common, num_ref, max_ref, den_ref)

    def merge(last):
        def body(c, _):
            rows = pl.ds(pl.multiple_of(c * ROW_CHUNK, ROW_CHUNK), ROW_CHUNK)
            m_old = max_ref[rows, :]
            m_grp = m_ref[rows, :]
            m_new = jnp.maximum(m_old, m_grp)
            a = jnp.exp(m_old - m_new)
            b = jnp.exp(m_grp - m_new)
            num = num_ref[rows, :] * a + acc_ref[rows, :] * b
            den = den_ref[rows, :] * a + s_ref[rows, :] * b
            if last:
                o_ref[rows, :] = (num / den).astype(o_ref.dtype)
            else:
                num_ref[rows, :] = num
                den_ref[rows, :] = den
                max_ref[rows, :] = m_new
            return 0
        lax.fori_loop(0, n_chunks, body, 0)

    for gi in (1, 2):
        @pl.when(g == gi)
        def _(gi=gi):
            _attn_group(ATTN_DILATIONS[gi], seq, *common, acc_ref, m_ref, s_ref)
            merge(last=(gi == N_GROUPS - 1))


def _attention(proj, cos2, sins, batch, seq):
    T = proj.shape[0]
    qb, kb, vb = COL_Q // HEAD_DIM, COL_K // HEAD_DIM, COL_VA // HEAD_DIM
    pad_rows = seq + max(ATTN_DILATIONS) * Q_BLOCK

    def head_spec(base):
        return pl.BlockSpec((seq, HEAD_DIM), lambda b, j, g: (b, base + g * HEADS_PER_GROUP + j))

    tab_spec = pl.BlockSpec((seq, HEAD_DIM), lambda b, j, g: (b, 0))
    big = pltpu.VMEM((seq, HEAD_DIM), F32)
    return pl.pallas_call(
        functools.partial(_attn_kernel, seq),
        grid=(batch, HEADS_PER_GROUP, N_GROUPS),
        in_specs=[head_spec(qb), head_spec(kb), head_spec(vb), tab_spec, tab_spec],
        out_specs=pl.BlockSpec((seq, HEAD_DIM), lambda b, j, g: (b, j)),
        out_shape=jax.ShapeDtypeStruct((T, ATTN_OUT_WIDTH), BF16),
        scratch_shapes=[big,
                        pltpu.VMEM((seq, HEAD_DIM), BF16),
                        pltpu.VMEM((pad_rows, HEAD_DIM), BF16),
                        pltpu.VMEM((pad_rows, HEAD_DIM), BF16),
                        big, big, big, big, big, big],
        compiler_params=_params(("arbitrary", "arbitrary", "arbitrary"), 48),
        name="dilated_attn",
    )(proj, proj, proj, cos2, sins)


def _mix_kernel(seq, ts,
                b_ref, c_ref, v_ref, ch_ref, vh_ref, gc_ref, ga_ref, ao_ref, x_ref,
                cw_ref, wc_ref, wa_ref, wo_ref, gate1_ref, g2_ref, sc2_ref, sh2_ref, wr_ref,
                x1_ref, h2_ref, lt_ref):
    i = pl.program_id(0)
    not_start = ((i * ts) % seq != 0).astype(F32)
    u = c_ref[...].astype(F32) * v_ref[...].astype(F32)
    halo = ch_ref[...].astype(F32) * vh_ref[...].astype(F32) * not_start
    h1 = halo[15:16, :]
    h2 = halo[14:15, :]
    row = lax.broadcasted_iota(jnp.int32, u.shape, 0)
    up1 = jnp.where(row == 0, h1, pltpu.roll(u, 1, 0))
    up2 = jnp.where(row == 0, h2, jnp.where(row == 1, h1, pltpu.roll(u, 2, 0)))
    cw = cw_ref[...]
    conv = cw[0:1, :] * up2 + cw[1:2, :] * up1 + cw[2:3, :] * u
    yc = jnp.dot((b_ref[...].astype(F32) * conv).astype(BF16), wc_ref[...], preferred_element_type=F32)
    ya = jnp.dot(ao_ref[...], wa_ref[...], preferred_element_type=F32)
    merged = _sigmoid(gc_ref[...].astype(F32)) * yc + _sigmoid(ga_ref[...].astype(F32)) * ya
    mo = jnp.dot(merged.astype(BF16), wo_ref[...], preferred_element_type=F32)
    x1 = x_ref[...] + gate1_ref[...] * mo
    x1_ref[...] = x1
    h2n = _modnorm(x1, g2_ref[...], sc2_ref[...], sh2_ref[...])
    _store_token_tiles(h2_ref, _pack_halves(h2n))
    lt_ref[...] = lax.dot_general(wr_ref[...], h2n, (((1,), (1,)), ((), ())),
                                  precision=lax.Precision.HIGHEST, preferred_element_type=F32)


def _mixer_out(proj, attn_o, x2d, conv_w, wc, wa, wo, mod3, g2, wr_t, seq):
    T, D = x2d.shape
    ts = 256
    per_seq = seq // ts
    halo_rows = 16

    def col(width, off):
        return pl.BlockSpec((ts, width), lambda i: (i, off // width))

    def halo(off):
        return pl.BlockSpec((halo_rows, CONV_WIDTH),
                            lambda i: (jnp.maximum(i * (ts // halo_rows) - 1, 0), off // CONV_WIDTH))

    def const(shape):
        return pl.BlockSpec(shape, lambda i: (0,) * len(shape), pipeline_mode=pl.Buffered(1))

    def mod(k):
        return pl.BlockSpec((None, 1, D), lambda i: ((i // per_seq) * 6 + k, 0, 0))

    return pl.pallas_call(
        functools.partial(_mix_kernel, seq, ts),
        grid=(T // ts,),
        in_specs=[col(CONV_WIDTH, COL_B), col(CONV_WIDTH, COL_C), col(CONV_WIDTH, COL_V),
                  halo(COL_C), halo(COL_V),
                  col(D, COL_GCONV), col(D, COL_GATTN),
                  pl.BlockSpec((ts, ATTN_OUT_WIDTH), lambda i: (i, 0)),
                  pl.BlockSpec((ts, D), lambda i: (i, 0)),
                  const((CONV_K, CONV_WIDTH)), const((CONV_WIDTH, D)), const((ATTN_OUT_WIDTH, D)),
                  const((D, D)),
                  mod(2), const((1, D)), mod(4), mod(3), const((N_EXPERTS, D))],
        out_specs=[pl.BlockSpec((ts, D), lambda i: (i, 0)),
                   pl.BlockSpec((ts * ROW_TILE, LANES), lambda i: (i, 0)),
                   pl.BlockSpec((N_EXPERTS, ts), lambda i: (0, i))],
        out_shape=[jax.ShapeDtypeStruct((T, D), F32),
                   jax.ShapeDtypeStruct((T * ROW_TILE, LANES), U32),
                   jax.ShapeDtypeStruct((N_EXPERTS, T), F32)],
        compiler_params=_params(("arbitrary",), 56),
        name="mixer_out",
    )(proj, proj, proj, proj, proj, proj, proj, attn_o, x2d,
      conv_w, wc, wa, wo, mod3, g2, mod3, mod3, wr_t)


def _route_kernel(lt_ref, bias_ref, eid_ref, rank_ref, w_ref, cnt_ref, carry_ref):
    @pl.when(pl.program_id(0) == 0)
    def _():
        carry_ref[...] = jnp.zeros_like(carry_ref)

    lt = lt_ref[...]
    tr = lt.shape[1]
    per = N_EXPERTS // N_EXPERT_GROUPS
    scores = _sigmoid(lt)
    sel = scores + bias_ref[...]
    sel3 = sel.reshape(N_EXPERT_GROUPS, per, tr)
    sub = lax.broadcasted_iota(jnp.int32, sel3.shape, 1)
    top1 = jnp.max(sel3, axis=1, keepdims=True)
    first = jnp.min(jnp.where(sel3 == top1, sub, per), axis=1, keepdims=True)
    top2 = jnp.max(jnp.where(sub == first, -jnp.inf, sel3), axis=1, keepdims=True)
    gs = (top1 + top2).reshape(N_EXPERT_GROUPS, tr)
    gidx = lax.broadcasted_iota(jnp.int32, gs.shape, 0)
    grank = jnp.zeros(gs.shape, jnp.int32)
    for other in range(N_EXPERT_GROUPS):
        row = gs[other:other + 1, :]
        grank += ((row > gs) | ((row == gs) & (other < gidx))).astype(jnp.int32)
    gmask = grank < TOP_GROUPS
    emask = jnp.broadcast_to(gmask[:, None, :], sel3.shape).reshape(N_EXPERTS, tr)
    cand = jnp.where(emask, sel, NEG_INF)
    eidx = lax.broadcasted_iota(jnp.int32, cand.shape, 0)
    hits, eids, ws = [], [], []
    for _ in range(TOP_K):
        mx = jnp.max(cand, axis=0, keepdims=True)
        idx = jnp.min(jnp.where(cand == mx, eidx, N_EXPERTS), axis=0, keepdims=True)
        hit = eidx == idx
        hits.append(hit)
        eids.append(idx)
        ws.append(jnp.sum(jnp.where(hit, scores, 0.0), axis=0, keepdims=True))
        cand = jnp.where(hit, -jnp.inf, cand)
    chosen = hits[0]
    for hit in hits[1:]:
        chosen = chosen | hit
    chosen = chosen.astype(F32)
    before = (lax.broadcasted_iota(jnp.int32, (tr, tr), 0) < lax.broadcasted_iota(jnp.int32, (tr, tr), 1))
    seen = jnp.dot(chosen.astype(BF16), before.astype(BF16), preferred_element_type=F32) + carry_ref[...]
    ranks = [jnp.sum(jnp.where(hit, seen, 0.0), axis=0, keepdims=True) for hit in hits]
    carry_ref[...] += jnp.sum(chosen, axis=1, keepdims=True)
    cnt_ref[...] = carry_ref[...].astype(jnp.int32)
    wsum = ws[0]
    for w in ws[1:]:
        wsum = wsum + w
    w8 = jnp.concatenate(ws, axis=0) / wsum * ROUTED_SCALE
    w_ref[...] = w8.T
    eid_ref[...] = jnp.concatenate(eids, axis=0)
    rank_ref[...] = jnp.concatenate(ranks, axis=0).astype(jnp.int32)


def _route(logits_t, router_bias):
    E, T = logits_t.shape
    tr = 512
    return pl.pallas_call(
        _route_kernel,
        grid=(T // tr,),
        in_specs=[pl.BlockSpec((E, tr), lambda i: (0, i)),
                  pl.BlockSpec((E, 1), lambda i: (0, 0))],
        out_specs=[pl.BlockSpec((TOP_K, tr), lambda i: (0, i)),
                   pl.BlockSpec((TOP_K, tr), lambda i: (0, i)),
                   pl.BlockSpec((tr, TOP_K), lambda i: (i, 0)),
                   pl.BlockSpec((E, 1), lambda i: (0, 0))],
        out_shape=[jax.ShapeDtypeStruct((TOP_K, T), jnp.int32),
                   jax.ShapeDtypeStruct((TOP_K, T), jnp.int32),
                   jax.ShapeDtypeStruct((T, TOP_K), F32),
                   jax.ShapeDtypeStruct((E, 1), jnp.int32)],
        scratch_shapes=[pltpu.VMEM((E, 1), F32)],
        compiler_params=_params(("arbitrary",), 32),
        name="router",
    )(logits_t, router_bias.reshape(E, 1))


def _slot_kernel(offs_ref, eid_ref, rank_ref, pos_ref):
    eid = eid_ref[...]
    pos = rank_ref[...]
    for e in range(N_EXPERTS):
        pos = pos + jnp.where(eid == e, offs_ref[e], 0)
    pos_ref[...] = pos * ROW_TILE


def _slots(offs, eid, rank):
    return pl.pallas_call(
        _slot_kernel,
        in_specs=[pl.BlockSpec(memory_space=pltpu.SMEM),
                  pl.BlockSpec(memory_space=pltpu.VMEM),
                  pl.BlockSpec(memory_space=pltpu.VMEM)],
        out_specs=pl.BlockSpec(memory_space=pltpu.VMEM),
        out_shape=jax.ShapeDtypeStruct(eid.shape, jnp.int32),
        name="slot_rows",
    )(offs, eid, rank)


def _tile_at(ref, row):
    return ref.at[pl.ds(pl.multiple_of(row, ROW_TILE), ROW_TILE)]


def _dispatch_kernel(tt, pos_ref, h_ref, xs_ref, sem):
    def body(t, _):
        src = _tile_at(h_ref, t * ROW_TILE)
        for k in range(TOP_K):
            pltpu.make_async_copy(src, _tile_at(xs_ref, pos_ref[t * TOP_K + k]), sem).start(priority=k % 2)
        return 0
    lax.fori_loop(0, tt, body, 0)
    for k in range(TOP_K):
        pltpu.make_async_copy(h_ref, xs_ref.at[pl.ds(0, tt * ROW_TILE)], sem).wait()


def _dispatch(pos, h2p):
    T = h2p.shape[0] // ROW_TILE
    tt = 512
    return pl.pallas_call(
        functools.partial(_dispatch_kernel, tt),
        grid=(T // tt,),
        in_specs=[pl.BlockSpec((tt * TOP_K,), lambda i: (i,), memory_space=pltpu.SMEM),
                  pl.BlockSpec((tt * ROW_TILE, LANES), lambda i: (i, 0))],
        out_specs=pl.BlockSpec(memory_space=pl.ANY),
        out_shape=jax.ShapeDtypeStruct((T * TOP_K * ROW_TILE, LANES), U32),
        scratch_shapes=[pltpu.SemaphoreType.DMA],
        compiler_params=_params(("arbitrary",), 32),
        name="dispatch",
    )(pos, h2p)


EXPERT_TILE = 256


def _work_items(counts, n_rows):
    tm = EXPERT_TILE
    n_tiles = n_rows // tm
    ends = jnp.cumsum(counts)
    offs = ends - counts
    cuts = jnp.sort(jnp.concatenate([jnp.arange(n_tiles, dtype=jnp.int32) * tm, offs]))
    nxt = jnp.concatenate([cuts[1:], jnp.array([n_rows], jnp.int32)])
    tile = jnp.minimum(cuts // tm, n_tiles - 1)
    expert = jnp.minimum(jnp.sum((ends[None, :] <= cuts[:, None]).astype(jnp.int32), axis=1), N_EXPERTS - 1)
    lo = cuts - tile * tm
    hi = nxt - tile * tm
    fresh = jnp.concatenate([jnp.ones((1,), jnp.int32), (expert[1:] != expert[:-1]).astype(jnp.int32)])
    return offs, tile, expert, lo, hi, fresh


def _experts_kernel(tile_ref, exp_ref, lo_ref, hi_ref, fresh_ref,
                    xs_ref, wg_ref, wu_ref, wd_ref, ys_ref, wgb_ref, wub_ref, wdb_ref, acc_ref):
    i = pl.program_id(0)
    lo = lo_ref[i]
    hi = hi_ref[i]

    @pl.when(fresh_ref[i] == 1)
    def _():
        wgb_ref[...] = wg_ref[...].astype(BF16)
        wub_ref[...] = wu_ref[...].astype(BF16)
        wdb_ref[...] = wd_ref[...].astype(BF16)

    @pl.when(lo == 0)
    def _():
        acc_ref[...] = jnp.zeros_like(acc_ref)

    @pl.when(hi > lo)
    def _():
        x = _unpack_rows_bf16(_load_token_tiles(xs_ref, acc_ref.shape[0]))
        a = jnp.dot(x, wgb_ref[...], preferred_element_type=F32)
        u = jnp.dot(x, wub_ref[...], preferred_element_type=F32)
        row = lax.broadcasted_iota(jnp.int32, (a.shape[0], 1), 0)
        h = jnp.where((row >= lo) & (row < hi), (a * _sigmoid(a)) * u, 0.0)
        acc_ref[...] += jnp.dot(h.astype(BF16), wdb_ref[...], preferred_element_type=F32)

    _store_token_tiles(ys_ref, _pack_halves(acc_ref[...]))


def _experts(items, xs, wg, wu, wd):
    n_rows = xs.shape[0] // ROW_TILE
    _, D, F = wg.shape
    tm = EXPERT_TILE
    n_items = items[0].shape[0]
    tiles = pl.BlockSpec((tm * ROW_TILE, LANES), lambda i, tile, exp, lo, hi, fr: (tile[i], 0))
    grid_spec = pltpu.PrefetchScalarGridSpec(
        num_scalar_prefetch=5,
        grid=(n_items,),
        in_specs=[tiles,
                  pl.BlockSpec((None, D, F), lambda i, tile, exp, lo, hi, fr: (exp[i], 0, 0)),
                  pl.BlockSpec((None, D, F), lambda i, tile, exp, lo, hi, fr: (exp[i], 0, 0)),
                  pl.BlockSpec((None, F, D), lambda i, tile, exp, lo, hi, fr: (exp[i], 0, 0))],
        out_specs=tiles,
        scratch_shapes=[pltpu.VMEM((D, F), BF16), pltpu.VMEM((D, F), BF16), pltpu.VMEM((F, D), BF16),
                        pltpu.VMEM((tm, D), F32)])
    return pl.pallas_call(
        _experts_kernel,
        grid_spec=grid_spec,
        out_shape=jax.ShapeDtypeStruct((n_rows * ROW_TILE, LANES), U32),
        compiler_params=_params(("arbitrary",), 56),
        name="routed_experts",
    )(*items, xs, wg, wu, wd)


def _combine_kernel(tt, pos_ref, w_ref, hp_ref, sg_ref, su_ref, sd_ref, x1_ref, gate2_ref, gf_ref,
                    ys_ref, y_ref, buf_ref, sem):
    def body(t, _):
        for k in range(TOP_K):
            pltpu.make_async_copy(_tile_at(ys_ref, pos_ref[t * TOP_K + k]), _tile_at(buf_ref.at[k], t * ROW_TILE),
                                  sem).start(priority=k % 2)
        return 0
    lax.fori_loop(0, tt, body, 0)

    x = _unpack_rows_bf16(_load_token_tiles(hp_ref, tt))
    a = jnp.dot(x, sg_ref[...], preferred_element_type=F32)
    u = jnp.dot(x, su_ref[...], preferred_element_type=F32)
    moe = jnp.dot(((a * _sigmoid(a)) * u).astype(BF16), sd_ref[...], preferred_element_type=F32)

    for k in range(TOP_K):
        pltpu.make_async_copy(ys_ref.at[pl.ds(0, tt * ROW_TILE)], buf_ref.at[k], sem).wait()
    half = moe.shape[1] // 2
    w = w_ref[...]
    lo_acc = moe[:, :half]
    hi_acc = moe[:, half:]
    for k in range(TOP_K):
        lo, hi = _unpack_halves(_load_token_tiles(buf_ref.at[k], tt))
        wk = w[:, k:k + 1]
        lo_acc = lo_acc + wk * lo
        hi_acc = hi_acc + wk * hi
    moe = jnp.concatenate([lo_acc, hi_acc], axis=1)
    x2 = x1_ref[...] + gate2_ref[...] * moe
    xf = x2 * lax.rsqrt(jnp.mean(x2 * x2, axis=-1, keepdims=True) + EPS)
    y_ref[...] = xf * gf_ref[...]


def _combine(pos, w, h2p, ys, sg, su, sd, x1, mod3, gf, seq):
    T, D = x1.shape
    assert D // 2 == ROW_TILE * LANES, "a packed token row must fill exactly one (8, 128) tile"
    tt = 256
    per_seq = seq // tt

    def const(shape):
        return pl.BlockSpec(shape, lambda i: (0,) * len(shape))

    return pl.pallas_call(
        functools.partial(_combine_kernel, tt),
        grid=(T // tt,),
        in_specs=[pl.BlockSpec((tt * TOP_K,), lambda i: (i,), memory_space=pltpu.SMEM),
                  pl.BlockSpec((tt, TOP_K), lambda i: (i, 0)),
                  pl.BlockSpec((tt * ROW_TILE, LANES), lambda i: (i, 0)),
                  const(sg.shape), const(su.shape), const(sd.shape),
                  pl.BlockSpec((tt, D), lambda i: (i, 0)),
                  pl.BlockSpec((None, 1, D), lambda i: ((i // per_seq) * 6 + 5, 0, 0)),
                  const((1, D)),
                  pl.BlockSpec(memory_space=pl.ANY)],
        out_specs=pl.BlockSpec((tt, D), lambda i: (i, 0)),
        out_shape=jax.ShapeDtypeStruct((T, D), F32),
        scratch_shapes=[pltpu.VMEM((TOP_K, tt * ROW_TILE, LANES), U32), pltpu.SemaphoreType.DMA],
        compiler_params=_params(("arbitrary",), 48),
        name="combine",
    )(pos, w, h2p, sg, su, sd, x1, mod3, gf, ys)


def kernel(x, c, positions, norm_mix_g, w_ada, b_ada, w_in, conv_w, w_conv_out, w_attn_out,
           w_o, norm_ffn_g, w_router, router_bias, w_exp_gate, w_exp_up, w_exp_down,
           w_sh_gate, w_sh_up, w_sh_down, norm_final_g):
    B, S, D = x.shape
    T = B * S
    assert w_ada.shape[0] == 1, "the final norm is fused into the single layer's combine kernel"
    l = 0
    x2d = x.reshape(T, D)
    mod3 = _ada(c, w_ada[l], b_ada[l]).reshape(B * 6, 1, D)
    h1 = _norm1(x2d, norm_mix_g[l].reshape(1, D), mod3, S)
    proj = _inproj(h1, w_in[l])
    cos2, sins = _rope_tables(positions)
    attn_o = _attention(proj, cos2, sins, B, S)
    x1, h2p, logits_t = _mixer_out(
        proj, attn_o, x2d, conv_w[l], w_conv_out[l].astype(BF16), w_attn_out[l].astype(BF16),
        w_o[l].astype(BF16), mod3, norm_ffn_g[l].reshape(1, D), w_router[l].T, S)
    eid, rank, gate_w, counts = _route(logits_t, router_bias[l])
    offs, *items = _work_items(counts.reshape(N_EXPERTS), T * TOP_K)
    pos = _slots(offs, eid, rank).T.reshape(T * TOP_K)
    xs = _dispatch(pos, h2p)
    ys = _experts(items, xs, w_exp_gate[l], w_exp_up[l], w_exp_down[l])
    y = _combine(pos, gate_w, h2p, ys, w_sh_gate[l].astype(BF16), w_sh_up[l].astype(BF16),
                 w_sh_down[l].astype(BF16), x1, mod3, norm_final_g.reshape(1, D), S)
    return y.reshape(B, S, D)
```

```python
import functools

import jax
import jax.numpy as jnp
from jax import lax
from jax.experimental import pallas as pl
from jax.experimental.pallas import tpu as pltpu

D_MODEL = 2048
HEAD_DIM = 128
ATTN_DILATIONS = (1, 4, 16)
N_GROUPS = 3
HEADS_PER_GROUP = 4
ATTN_WIDTH = N_GROUPS * HEADS_PER_GROUP * HEAD_DIM
ATTN_OUT_WIDTH = HEADS_PER_GROUP * HEAD_DIM
ROPE_THETA = 10000.0
Q_BLOCK = 128
CONV_WIDTH = D_MODEL // 2
CONV_K = 3
IN_COLS = 3 * CONV_WIDTH + 3 * ATTN_WIDTH + 2 * D_MODEL
N_EXPERTS = 64
TOP_K = 8
N_EXPERT_GROUPS = 8
TOP_GROUPS = 4
D_EXPERT = D_MODEL // 4
ROUTED_SCALE = 2.5
EPS = 1e-6
NEG_INF = -1e30

F32 = jnp.float32
BF16 = jnp.bfloat16
U32 = jnp.uint32
MIB = 1024 * 1024

PROJ_BLOCK = 512
N_PROJ_BLOCKS = IN_COLS // PROJ_BLOCK
GATE_SRC_BLOCK = (3 * CONV_WIDTH + 3 * ATTN_WIDTH) // PROJ_BLOCK
N_GATE_BLOCKS = 2 * D_MODEL // PROJ_BLOCK
COL_GCONV = 0
COL_GATTN = D_MODEL
COL_B = 2 * D_MODEL
COL_C = COL_B + CONV_WIDTH
COL_V = COL_C + CONV_WIDTH
COL_Q = COL_V + CONV_WIDTH
COL_K = COL_Q + ATTN_WIDTH
COL_VA = COL_K + ATTN_WIDTH


def _params(semantics, vmem_mib):
    return pltpu.CompilerParams(dimension_semantics=semantics, vmem_limit_bytes=vmem_mib * MIB)


def _sigmoid(x):
    return 1.0 / (1.0 + jnp.exp(-x))


def _pack_halves(x):
    n = x.shape[1] // 2
    return pltpu.pack_elementwise([x[:, :n], x[:, n:]], packed_dtype=BF16)


def _unpack_halves(xp):
    lo = pltpu.unpack_elementwise(xp, index=0, packed_dtype=BF16, unpacked_dtype=F32)
    hi = pltpu.unpack_elementwise(xp, index=1, packed_dtype=BF16, unpacked_dtype=F32)
    return lo, hi


def _unpack_rows_bf16(xp):
    lo, hi = _unpack_halves(xp)
    return jnp.concatenate([lo.astype(BF16), hi.astype(BF16)], axis=1)


ROW_TILE = 8
LANES = 128


def _store_token_tiles(ref, packed):
    n = packed.shape[0]
    for c in range(ROW_TILE):
        ref[pl.ds(c, n, stride=ROW_TILE), :] = packed[:, c * LANES:(c + 1) * LANES]


def _load_token_tiles(ref, n):
    return jnp.concatenate([ref[pl.ds(c, n, stride=ROW_TILE), :] for c in range(ROW_TILE)], axis=1)


def _ada_kernel(c_ref, w_ref, b_ref, o_ref):
    c = c_ref[...]
    s = c * _sigmoid(c)
    o_ref[...] = jnp.dot(s, w_ref[...], precision=lax.Precision.HIGHEST,
                         preferred_element_type=F32) + b_ref[...]


def _ada(c, w_ada, b_ada):
    B, D = c.shape
    N = w_ada.shape[1]
    tn = 1024
    return pl.pallas_call(
        _ada_kernel,
        grid=(N // tn,),
        in_specs=[pl.BlockSpec((B, D), lambda j: (0, 0)),
                  pl.BlockSpec((D, tn), lambda j: (0, j)),
                  pl.BlockSpec((1, tn), lambda j: (0, j))],
        out_specs=pl.BlockSpec((B, tn), lambda j: (0, j)),
        out_shape=jax.ShapeDtypeStruct((B, N), F32),
        compiler_params=_params(("arbitrary",), 40),
        name="ada_mod",
    )(c, w_ada, b_ada.reshape(1, N))


def _modnorm(x, g, scale, shift):
    xf = x * lax.rsqrt(jnp.mean(x * x, axis=-1, keepdims=True) + EPS)
    return (xf * g) * (1.0 + scale) + shift


def _norm1_kernel(x_ref, g_ref, sc_ref, sh_ref, o_ref):
    o_ref[...] = _modnorm(x_ref[...], g_ref[...], sc_ref[...], sh_ref[...]).astype(o_ref.dtype)


def _norm1(x2d, g, mod3, seq):
    T, D = x2d.shape
    tm = 512
    per_seq = seq // tm
    return pl.pallas_call(
        _norm1_kernel,
        grid=(T // tm,),
        in_specs=[pl.BlockSpec((tm, D), lambda i: (i, 0)),
                  pl.BlockSpec((1, D), lambda i: (0, 0)),
                  pl.BlockSpec((None, 1, D), lambda i: ((i // per_seq) * 6 + 1, 0, 0)),
                  pl.BlockSpec((None, 1, D), lambda i: ((i // per_seq) * 6 + 0, 0, 0))],
        out_specs=pl.BlockSpec((tm, D), lambda i: (i, 0)),
        out_shape=jax.ShapeDtypeStruct((T, D), BF16),
        compiler_params=_params(("arbitrary",), 32),
        name="norm_mix",
    )(x2d, g, mod3, mod3)


def _inproj_kernel(h_ref, w_ref, o_ref, wbf_ref):
    @pl.when(pl.program_id(1) == 0)
    def _():
        wbf_ref[...] = w_ref[...].astype(BF16)

    o_ref[...] = jnp.dot(h_ref[...], wbf_ref[...], preferred_element_type=F32).astype(o_ref.dtype)


def _proj_dst_block(j):
    return jnp.where(j < GATE_SRC_BLOCK, j + N_GATE_BLOCKS, j - GATE_SRC_BLOCK)


def _inproj(h, w_in):
    T, D = h.shape
    tm = 2048
    tn = PROJ_BLOCK
    return pl.pallas_call(
        _inproj_kernel,
        grid=(N_PROJ_BLOCKS, T // tm),
        in_specs=[pl.BlockSpec((tm, D), lambda j, i: (i, 0)),
                  pl.BlockSpec((D, tn), lambda j, i: (0, j))],
        out_specs=pl.BlockSpec((tm, tn), lambda j, i: (i, _proj_dst_block(j))),
        out_shape=jax.ShapeDtypeStruct((T, IN_COLS), BF16),
        scratch_shapes=[pltpu.VMEM((D, tn), BF16)],
        compiler_params=_params(("arbitrary", "arbitrary"), 48),
        name="in_proj",
    )(h, w_in)


def _rope_kernel(pos_ref, inv_ref, sign_ref, cos_ref, sin_ref):
    ang = pos_ref[...].astype(F32) * inv_ref[...]
    cos_ref[...] = jnp.cos(ang)
    sin_ref[...] = jnp.sin(ang) * sign_ref[...]


def _rope_tables(positions):
    T = positions.size
    ts = 1024
    inv = ROPE_THETA ** (-jnp.arange(0, HEAD_DIM, 2, dtype=F32) / HEAD_DIM)
    inv2 = jnp.concatenate([inv, inv]).reshape(1, HEAD_DIM)
    sign = jnp.concatenate([-jnp.ones((HEAD_DIM // 2,), F32), jnp.ones((HEAD_DIM // 2,), F32)]).reshape(1, HEAD_DIM)
    return pl.pallas_call(
        _rope_kernel,
        grid=(T // ts,),
        in_specs=[pl.BlockSpec((ts, 1), lambda i: (i, 0)),
                  pl.BlockSpec((1, HEAD_DIM), lambda i: (0, 0)),
                  pl.BlockSpec((1, HEAD_DIM), lambda i: (0, 0))],
        out_specs=[pl.BlockSpec((ts, HEAD_DIM), lambda i: (i, 0)),
                   pl.BlockSpec((ts, HEAD_DIM), lambda i: (i, 0))],
        out_shape=[jax.ShapeDtypeStruct((T, HEAD_DIM), F32)] * 2,
        compiler_params=_params(("arbitrary",), 32),
        name="rope_tables",
    )(positions.reshape(T, 1), inv2, sign)


ROW_CHUNK = 512


def _attn_group(r, seq, q_ref, k_ref, v_ref, cos_ref, sin_ref, nat_ref, qd_ref, kd_ref, vd_ref,
                acc_ref, m_ref, s_ref):
    n_sub = seq // r
    nb = n_sub // Q_BLOCK
    cls = n_sub + Q_BLOCK
    n_chunks = seq // ROW_CHUNK
    scale = HEAD_DIM ** -0.5

    def rope(t_ref, c0, mult):
        t = t_ref[pl.ds(c0, ROW_CHUNK), :].astype(F32)
        cs = cos_ref[pl.ds(c0, ROW_CHUNK), :]
        sn = sin_ref[pl.ds(c0, ROW_CHUNK), :]
        out = t * cs + pltpu.roll(t, HEAD_DIM // 2, 1) * sn
        return out * mult if mult is not None else out

    def deinterleave(dst_ref, dst_stride, dst_off):
        for rho in range(r):
            dst_ref[pl.ds(rho * dst_stride + dst_off, n_sub), :] = (
                nat_ref[pl.ds(rho, n_sub, stride=r), :].astype(BF16))

    zeros_blk = jnp.zeros((Q_BLOCK, HEAD_DIM), BF16)
    for rho in range(r):
        kd_ref[pl.ds(rho * cls, Q_BLOCK), :] = zeros_blk
        vd_ref[pl.ds(rho * cls, Q_BLOCK), :] = zeros_blk

    if r == 1:
        def fill(c, _):
            c0 = pl.multiple_of(c * ROW_CHUNK, ROW_CHUNK)
            qd_ref[pl.ds(c0, ROW_CHUNK), :] = rope(q_ref, c0, scale).astype(BF16)
            kd_ref[pl.ds(Q_BLOCK + c0, ROW_CHUNK), :] = rope(k_ref, c0, None).astype(BF16)
            vd_ref[pl.ds(Q_BLOCK + c0, ROW_CHUNK), :] = v_ref[pl.ds(c0, ROW_CHUNK), :]
            return 0
        lax.fori_loop(0, n_chunks, fill, 0)
    else:
        def fill_q(c, _):
            c0 = pl.multiple_of(c * ROW_CHUNK, ROW_CHUNK)
            nat_ref[pl.ds(c0, ROW_CHUNK), :] = rope(q_ref, c0, scale)
            return 0
        lax.fori_loop(0, n_chunks, fill_q, 0)
        deinterleave(qd_ref, n_sub, 0)

        def fill_k(c, _):
            c0 = pl.multiple_of(c * ROW_CHUNK, ROW_CHUNK)
            nat_ref[pl.ds(c0, ROW_CHUNK), :] = rope(k_ref, c0, None)
            return 0
        lax.fori_loop(0, n_chunks, fill_k, 0)
        deinterleave(kd_ref, cls, Q_BLOCK)

        def fill_v(c, _):
            c0 = pl.multiple_of(c * ROW_CHUNK, ROW_CHUNK)
            nat_ref[pl.ds(c0, ROW_CHUNK), :] = v_ref[pl.ds(c0, ROW_CHUNK), :].astype(F32)
            return 0
        lax.fori_loop(0, n_chunks, fill_v, 0)
        deinterleave(vd_ref, cls, Q_BLOCK)

    qi = lax.broadcasted_iota(jnp.int32, (Q_BLOCK, 2 * Q_BLOCK), 0)
    kj = lax.broadcasted_iota(jnp.int32, (Q_BLOCK, 2 * Q_BLOCK), 1)
    cur_ok = (kj >= Q_BLOCK) & (kj - Q_BLOCK <= qi)
    prev_ok = (kj < Q_BLOCK) & (kj >= qi)

    def block(blk, _):
        rho = blk // nb
        n = blk % nb
        qrow = pl.multiple_of(rho * n_sub + n * Q_BLOCK, Q_BLOCK)
        krow = pl.multiple_of(rho * cls + n * Q_BLOCK, Q_BLOCK)
        q = qd_ref[pl.ds(qrow, Q_BLOCK), :]
        kw = kd_ref[pl.ds(krow, 2 * Q_BLOCK), :]
        vw = vd_ref[pl.ds(krow, 2 * Q_BLOCK), :]
        s = lax.dot_general(q, kw, (((1,), (1,)), ((), ())), preferred_element_type=F32)
        ok = cur_ok | (prev_ok & (n > 0))
        s = jnp.where(ok, s, NEG_INF)
        m = jnp.max(s, axis=1, keepdims=True)
        p = jnp.exp(s - m)
        ssum = jnp.sum(p, axis=1, keepdims=True)
        acc = jnp.dot(p.astype(BF16), vw, preferred_element_type=F32)
        start = rho + n * (Q_BLOCK * r)
        if r == 1:
            rows = pl.ds(pl.multiple_of(start, Q_BLOCK), Q_BLOCK)
        else:
            rows = pl.ds(start, Q_BLOCK, stride=r)
        acc_ref[rows, :] = acc
        m_ref[rows, :] = jnp.broadcast_to(m, (Q_BLOCK, HEAD_DIM))
        s_ref[rows, :] = jnp.broadcast_to(ssum, (Q_BLOCK, HEAD_DIM))
        return 0

    lax.fori_loop(0, r * nb, block, 0, unroll=8)


def _attn_kernel(seq, q_ref, k_ref, v_ref, cos_ref, sin_ref, o_ref,
                 nat_ref, qd_ref, kd_ref, vd_ref, num_ref, den_ref, max_ref, acc_ref, m_ref, s_ref):
    g = pl.program_id(2)
    n_chunks = seq // ROW_CHUNK
    common = (q_ref, k_ref, v_ref, cos_ref, sin_ref, nat_ref, qd_ref, kd_ref, vd_ref)

    @pl.when(g == 0)
    def _():
        _attn_group(ATTN_DILATIONS[0], seq, *common, num_ref, max_ref, den_ref)

    def merge(last):
        def body(c, _):
            rows = pl.ds(pl.multiple_of(c * ROW_CHUNK, ROW_CHUNK), ROW_CHUNK)
            m_old = max_ref[rows, :]
            m_grp = m_ref[rows, :]
            m_new = jnp.maximum(m_old, m_grp)
            a = jnp.exp(m_old - m_new)
            b = jnp.exp(m_grp - m_new)
            num = num_ref[rows, :] * a + acc_ref[rows, :] * b
            den = den_ref[rows, :] * a + s_ref[rows, :] * b
            if last:
                o_ref[rows, :] = (num / den).astype(o_ref.dtype)
            else:
                num_ref[rows, :] = num
                den_ref[rows, :] = den
                max_ref[rows, :] = m_new
            return 0
        lax.fori_loop(0, n_chunks, body, 0)

    for gi in (1, 2):
        @pl.when(g == gi)
        def _(gi=gi):
            _attn_group(ATTN_DILATIONS[gi], seq, *common, acc_ref, m_ref, s_ref)
            merge(last=(gi == N_GROUPS - 1))


def _attention(proj, cos2, sins, batch, seq):
    T = proj.shape[0]
    qb, kb, vb = COL_Q // HEAD_DIM, COL_K // HEAD_DIM, COL_VA // HEAD_DIM
    pad_rows = seq + max(ATTN_DILATIONS) * Q_BLOCK

    def head_spec(base):
        return pl.BlockSpec((seq, HEAD_DIM), lambda b, j, g: (b, base + g * HEADS_PER_GROUP + j))

    tab_spec = pl.BlockSpec((seq, HEAD_DIM), lambda b, j, g: (b, 0))
    big = pltpu.VMEM((seq, HEAD_DIM), F32)
    return pl.pallas_call(
        functools.partial(_attn_kernel, seq),
        grid=(batch, HEADS_PER_GROUP, N_GROUPS),
        in_specs=[head_spec(qb), head_spec(kb), head_spec(vb), tab_spec, tab_spec],
        out_specs=pl.BlockSpec((seq, HEAD_DIM), lambda b, j, g: (b, j)),
        out_shape=jax.ShapeDtypeStruct((T, ATTN_OUT_WIDTH), BF16),
        scratch_shapes=[big,
                        pltpu.VMEM((seq, HEAD_DIM), BF16),
                        pltpu.VMEM((pad_rows, HEAD_DIM), BF16),
                        pltpu.VMEM((pad_rows, HEAD_DIM), BF16),
                        big, big, big, big, big, big],
        compiler_params=_params(("arbitrary", "arbitrary", "arbitrary"), 48),
        name="dilated_attn",
    )(proj, proj, proj, cos2, sins)


def _mix_kernel(seq, ts,
                b_ref, c_ref, v_ref, ch_ref, vh_ref, gc_ref, ga_ref, ao_ref, x_ref,
                cw_ref, wc_ref, wa_ref, wo_ref, gate1_ref, g2_ref, sc2_ref, sh2_ref, wr_ref,
                x1_ref, h2_ref, lt_ref):
    i = pl.program_id(0)
    not_start = ((i * ts) % seq != 0).astype(F32)
    u = c_ref[...].astype(F32) * v_ref[...].astype(F32)
    halo = ch_ref[...].astype(F32) * vh_ref[...].astype(F32) * not_start
    h1 = halo[15:16, :]
    h2 = halo[14:15, :]
    row = lax.broadcasted_iota(jnp.int32, u.shape, 0)
    up1 = jnp.where(row == 0, h1, pltpu.roll(u, 1, 0))
    up2 = jnp.where(row == 0, h2, jnp.where(row == 1, h1, pltpu.roll(u, 2, 0)))
    cw = cw_ref[...]
    conv = cw[0:1, :] * up2 + cw[1:2, :] * up1 + cw[2:3, :] * u
    yc = jnp.dot((b_ref[...].astype(F32) * conv).astype(BF16), wc_ref[...], preferred_element_type=F32)
    ya = jnp.dot(ao_ref[...], wa_ref[...], preferred_element_type=F32)
    merged = _sigmoid(gc_ref[...].astype(F32)) * yc + _sigmoid(ga_ref[...].astype(F32)) * ya
    mo = jnp.dot(merged.astype(BF16), wo_ref[...], preferred_element_type=F32)
    x1 = x_ref[...] + gate1_ref[...] * mo
    x1_ref[...] = x1
    h2n = _modnorm(x1, g2_ref[...], sc2_ref[...], sh2_ref[...])
    _store_token_tiles(h2_ref, _pack_halves(h2n))
    lt_ref[...] = lax.dot_general(wr_ref[...], h2n, (((1,), (1,)), ((), ())),
                                  precision=lax.Precision.HIGHEST, preferred_element_type=F32)


def _mixer_out(proj, attn_o, x2d, conv_w, wc, wa, wo, mod3, g2, wr_t, seq):
    T, D = x2d.shape
    ts = 256
    per_seq = seq // ts
    halo_rows = 16

    def col(width, off):
        return pl.BlockSpec((ts, width), lambda i: (i, off // width))

    def halo(off):
        return pl.BlockSpec((halo_rows, CONV_WIDTH),
                            lambda i: (jnp.maximum(i * (ts // halo_rows) - 1, 0), off // CONV_WIDTH))

    def const(shape):
        return pl.BlockSpec(shape, lambda i: (0,) * len(shape), pipeline_mode=pl.Buffered(1))

    def mod(k):
        return pl.BlockSpec((None, 1, D), lambda i: ((i // per_seq) * 6 + k, 0, 0))

    return pl.pallas_call(
        functools.partial(_mix_kernel, seq, ts),
        grid=(T // ts,),
        in_specs=[col(CONV_WIDTH, COL_B), col(CONV_WIDTH, COL_C), col(CONV_WIDTH, COL_V),
                  halo(COL_C), halo(COL_V),
                  col(D, COL_GCONV), col(D, COL_GATTN),
                  pl.BlockSpec((ts, ATTN_OUT_WIDTH), lambda i: (i, 0)),
                  pl.BlockSpec((ts, D), lambda i: (i, 0)),
                  const((CONV_K, CONV_WIDTH)), const((CONV_WIDTH, D)), const((ATTN_OUT_WIDTH, D)),
                  const((D, D)),
                  mod(2), const((1, D)), mod(4), mod(3), const((N_EXPERTS, D))],
        out_specs=[pl.BlockSpec((ts, D), lambda i: (i, 0)),
                   pl.BlockSpec((ts * ROW_TILE, LANES), lambda i: (i, 0)),
                   pl.BlockSpec((N_EXPERTS, ts), lambda i: (0, i))],
        out_shape=[jax.ShapeDtypeStruct((T, D), F32),
                   jax.ShapeDtypeStruct((T * ROW_TILE, LANES), U32),
                   jax.ShapeDtypeStruct((N_EXPERTS, T), F32)],
        compiler_params=_params(("arbitrary",), 56),
        name="mixer_out",
    )(proj, proj, proj, proj, proj, proj, proj, attn_o, x2d,
      conv_w, wc, wa, wo, mod3, g2, mod3, mod3, wr_t)


def _route_kernel(lt_ref, bias_ref, eid_ref, rank_ref, w_ref, cnt_ref, carry_ref):
    @pl.when(pl.program_id(0) == 0)
    def _():
        carry_ref[...] = jnp.zeros_like(carry_ref)

    lt = lt_ref[...]
    tr = lt.shape[1]
    per = N_EXPERTS // N_EXPERT_GROUPS
    scores = _sigmoid(lt)
    sel = scores + bias_ref[...]
    sel3 = sel.reshape(N_EXPERT_GROUPS, per, tr)
    sub = lax.broadcasted_iota(jnp.int32, sel3.shape, 1)
    top1 = jnp.max(sel3, axis=1, keepdims=True)
    first = jnp.min(jnp.where(sel3 == top1, sub, per), axis=1, keepdims=True)
    top2 = jnp.max(jnp.where(sub == first, -jnp.inf, sel3), axis=1, keepdims=True)
    gs = (top1 + top2).reshape(N_EXPERT_GROUPS, tr)
    gidx = lax.broadcasted_iota(jnp.int32, gs.shape, 0)
    grank = jnp.zeros(gs.shape, jnp.int32)
    for other in range(N_EXPERT_GROUPS):
        row = gs[other:other + 1, :]
        grank += ((row > gs) | ((row == gs) & (other < gidx))).astype(jnp.int32)
    gmask = grank < TOP_GROUPS
    emask = jnp.broadcast_to(gmask[:, None, :], sel3.shape).reshape(N_EXPERTS, tr)
    cand = jnp.where(emask, sel, NEG_INF)
    eidx = lax.broadcasted_iota(jnp.int32, cand.shape, 0)
    hits, eids, ws = [], [], []
    for _ in range(TOP_K):
        mx = jnp.max(cand, axis=0, keepdims=True)
        idx = jnp.min(jnp.where(cand == mx, eidx, N_EXPERTS), axis=0, keepdims=True)
        hit = eidx == idx
        hits.append(hit)
        eids.append(idx)
        ws.append(jnp.sum(jnp.where(hit, scores, 0.0), axis=0, keepdims=True))
        cand = jnp.where(hit, -jnp.inf, cand)
    chosen = hits[0]
    for hit in hits[1:]:
        chosen = chosen | hit
    chosen = chosen.astype(F32)
    before = (lax.broadcasted_iota(jnp.int32, (tr, tr), 0) < lax.broadcasted_iota(jnp.int32, (tr, tr), 1))
    seen = jnp.dot(chosen.astype(BF16), before.astype(BF16), preferred_element_type=F32) + carry_ref[...]
    ranks = [jnp.sum(jnp.where(hit, seen, 0.0), axis=0, keepdims=True) for hit in hits]
    carry_ref[...] += jnp.sum(chosen, axis=1, keepdims=True)
    cnt_ref[...] = carry_ref[...].astype(jnp.int32)
    wsum = ws[0]
    for w in ws[1:]:
        wsum = wsum + w
    w8 = jnp.concatenate(ws, axis=0) / wsum * ROUTED_SCALE
    w_ref[...] = w8.T
    eid_ref[...] = jnp.concatenate(eids, axis=0)
    rank_ref[...] = jnp.concatenate(ranks, axis=0).astype(jnp.int32)


def _route(logits_t, router_bias):
    E, T = logits_t.shape
    tr = 512
    return pl.pallas_call(
        _route_kernel,
        grid=(T // tr,),
        in_specs=[pl.BlockSpec((E, tr), lambda i: (0, i)),
                  pl.BlockSpec((E, 1), lambda i: (0, 0))],
        out_specs=[pl.BlockSpec((TOP_K, tr), lambda i: (0, i)),
                   pl.BlockSpec((TOP_K, tr), lambda i: (0, i)),
                   pl.BlockSpec((tr, TOP_K), lambda i: (i, 0)),
                   pl.BlockSpec((E, 1), lambda i: (0, 0))],
        out_shape=[jax.ShapeDtypeStruct((TOP_K, T), jnp.int32),
                   jax.ShapeDtypeStruct((TOP_K, T), jnp.int32),
                   jax.ShapeDtypeStruct((T, TOP_K), F32),
                   jax.ShapeDtypeStruct((E, 1), jnp.int32)],
        scratch_shapes=[pltpu.VMEM((E, 1), F32)],
        compiler_params=_params(("arbitrary",), 32),
        name="router",
    )(logits_t, router_bias.reshape(E, 1))


def _slot_kernel(offs_ref, eid_ref, rank_ref, pos_ref):
    eid = eid_ref[...]
    pos = rank_ref[...]
    for e in range(N_EXPERTS):
        pos = pos + jnp.where(eid == e, offs_ref[e], 0)
    pos_ref[...] = pos * ROW_TILE


def _slots(offs, eid, rank):
    return pl.pallas_call(
        _slot_kernel,
        in_specs=[pl.BlockSpec(memory_space=pltpu.SMEM),
                  pl.BlockSpec(memory_space=pltpu.VMEM),
                  pl.BlockSpec(memory_space=pltpu.VMEM)],
        out_specs=pl.BlockSpec(memory_space=pltpu.VMEM),
        out_shape=jax.ShapeDtypeStruct(eid.shape, jnp.int32),
        name="slot_rows",
    )(offs, eid, rank)


def _tile_at(ref, row):
    return ref.at[pl.ds(pl.multiple_of(row, ROW_TILE), ROW_TILE)]


def _dispatch_kernel(tt, pos_ref, h_ref, xs_ref, sem):
    def body(t, _):
        src = _tile_at(h_ref, t * ROW_TILE)
        for k in range(TOP_K):
            pltpu.make_async_copy(src, _tile_at(xs_ref, pos_ref[t * TOP_K + k]), sem).start(priority=k % 2)
        return 0
    lax.fori_loop(0, tt, body, 0)
    for k in range(TOP_K):
        pltpu.make_async_copy(h_ref, xs_ref.at[pl.ds(0, tt * ROW_TILE)], sem).wait()


def _dispatch(pos, h2p):
    T = h2p.shape[0] // ROW_TILE
    tt = 512
    return pl.pallas_call(
        functools.partial(_dispatch_kernel, tt),
        grid=(T // tt,),
        in_specs=[pl.BlockSpec((tt * TOP_K,), lambda i: (i,), memory_space=pltpu.SMEM),
                  pl.BlockSpec((tt * ROW_TILE, LANES), lambda i: (i, 0))],
        out_specs=pl.BlockSpec(memory_space=pl.ANY),
        out_shape=jax.ShapeDtypeStruct((T * TOP_K * ROW_TILE, LANES), U32),
        scratch_shapes=[pltpu.SemaphoreType.DMA],
        compiler_params=_params(("arbitrary",), 32),
        name="dispatch",
    )(pos, h2p)


EXPERT_TILE = 256


def _work_items(counts, n_rows):
    tm = EXPERT_TILE
    n_tiles = n_rows // tm
    ends = jnp.cumsum(counts)
    offs = ends - counts
    cuts = jnp.sort(jnp.concatenate([jnp.arange(n_tiles, dtype=jnp.int32) * tm, offs]))
    nxt = jnp.concatenate([cuts[1:], jnp.array([n_rows], jnp.int32)])
    tile = jnp.minimum(cuts // tm, n_tiles - 1)
    expert = jnp.minimum(jnp.sum((ends[None, :] <= cuts[:, None]).astype(jnp.int32), axis=1), N_EXPERTS - 1)
    lo = cuts - tile * tm
    hi = nxt - tile * tm
    fresh = jnp.concatenate([jnp.ones((1,), jnp.int32), (expert[1:] != expert[:-1]).astype(jnp.int32)])
    n = cuts.shape[0]
    slot = (jnp.cumsum(fresh) - 1) % 2
    fresh_at = jnp.where(fresh == 1, jnp.arange(n, dtype=jnp.int32), n)
    later = jnp.concatenate([lax.cummin(fresh_at, axis=0, reverse=True)[1:], jnp.array([n], jnp.int32)])
    upcoming = jnp.where(later < n, expert[jnp.minimum(later, n - 1)], -1)
    return offs, tile, expert, lo, hi, fresh, slot, upcoming


def _experts_kernel(tile_ref, exp_ref, lo_ref, hi_ref, fresh_ref, slot_ref, upcoming_ref,
                    xs_ref, wg_hbm, wu_hbm, wd_hbm, ys_ref,
                    wg_st, wu_st, wd_st, wgb_ref, wub_ref, wdb_ref, acc_ref, sems):
    i = pl.program_id(0)
    lo = lo_ref[i]
    hi = hi_ref[i]

    def weight_copies(e, s):
        return (pltpu.make_async_copy(wg_hbm.at[e], wg_st.at[s], sems.at[s, 0]),
                pltpu.make_async_copy(wu_hbm.at[e], wu_st.at[s], sems.at[s, 1]),
                pltpu.make_async_copy(wd_hbm.at[e], wd_st.at[s], sems.at[s, 2]))

    @pl.when(fresh_ref[i] == 1)
    def _():
        s = slot_ref[i]

        @pl.when(i == 0)
        def _():
            for cp in weight_copies(exp_ref[i], s):
                cp.start()

        for cp in weight_copies(exp_ref[i], s):
            cp.wait()
        wgb_ref[...] = wg_st[s].astype(BF16)
        wub_ref[...] = wu_st[s].astype(BF16)
        wdb_ref[...] = wd_st[s].astype(BF16)

        @pl.when(upcoming_ref[i] >= 0)
        def _():
            for cp in weight_copies(upcoming_ref[i], 1 - s):
                cp.start()

    def piece():
        x = _unpack_rows_bf16(_load_token_tiles(xs_ref, acc_ref.shape[0]))
        a = jnp.dot(x, wgb_ref[...], preferred_element_type=F32)
        u = jnp.dot(x, wub_ref[...], preferred_element_type=F32)
        row = lax.broadcasted_iota(jnp.int32, (a.shape[0], 1), 0)
        h = jnp.where((row >= lo) & (row < hi), (a * _sigmoid(a)) * u, 0.0)
        return jnp.dot(h.astype(BF16), wdb_ref[...], preferred_element_type=F32)

    @pl.when((hi > lo) & (lo == 0))
    def _():
        acc_ref[...] = piece()

    @pl.when((hi > lo) & (lo > 0))
    def _():
        acc_ref[...] += piece()

    @pl.when((hi <= lo) & (lo == 0))
    def _():
        acc_ref[...] = jnp.zeros_like(acc_ref)

    _store_token_tiles(ys_ref, _pack_halves(acc_ref[...]))


def _experts(items, xs, wg, wu, wd):
    n_rows = xs.shape[0] // ROW_TILE
    _, D, F = wg.shape
    tm = EXPERT_TILE
    n_items = items[0].shape[0]
    tiles = pl.BlockSpec((tm * ROW_TILE, LANES), lambda i, tile, *_: (tile[i], 0))
    hbm = pl.BlockSpec(memory_space=pl.ANY)
    grid_spec = pltpu.PrefetchScalarGridSpec(
        num_scalar_prefetch=len(items),
        grid=(n_items,),
        in_specs=[tiles, hbm, hbm, hbm],
        out_specs=tiles,
        scratch_shapes=[pltpu.VMEM((2, D, F), F32), pltpu.VMEM((2, D, F), F32), pltpu.VMEM((2, F, D), F32),
                        pltpu.VMEM((D, F), BF16), pltpu.VMEM((D, F), BF16), pltpu.VMEM((F, D), BF16),
                        pltpu.VMEM((tm, D), F32), pltpu.SemaphoreType.DMA((2, 3))])
    return pl.pallas_call(
        _experts_kernel,
        grid_spec=grid_spec,
        out_shape=jax.ShapeDtypeStruct((n_rows * ROW_TILE, LANES), U32),
        compiler_params=_params(("arbitrary",), 56),
        name="routed_experts",
    )(*items, xs, wg, wu, wd)


def _combine_kernel(tt, pos_ref, w_ref, hp_ref, sg_ref, su_ref, sd_ref, x1_ref, gate2_ref, gf_ref,
                    ys_ref, y_ref, buf_ref, sem):
    def body(t, _):
        for k in range(TOP_K):
            pltpu.make_async_copy(_tile_at(ys_ref, pos_ref[t * TOP_K + k]), _tile_at(buf_ref.at[k], t * ROW_TILE),
                                  sem).start(priority=k % 2)
        return 0
    lax.fori_loop(0, tt, body, 0)

    x = _unpack_rows_bf16(_load_token_tiles(hp_ref, tt))
    a = jnp.dot(x, sg_ref[...], preferred_element_type=F32)
    u = jnp.dot(x, su_ref[...], preferred_element_type=F32)
    moe = jnp.dot(((a * _sigmoid(a)) * u).astype(BF16), sd_ref[...], preferred_element_type=F32)

    for k in range(TOP_K):
        pltpu.make_async_copy(ys_ref.at[pl.ds(0, tt * ROW_TILE)], buf_ref.at[k], sem).wait()
    half = moe.shape[1] // 2
    w = w_ref[...]
    lo_acc = moe[:, :half]
    hi_acc = moe[:, half:]
    for k in range(TOP_K):
        lo, hi = _unpack_halves(_load_token_tiles(buf_ref.at[k], tt))
        wk = w[:, k:k + 1]
        lo_acc = lo_acc + wk * lo
        hi_acc = hi_acc + wk * hi
    moe = jnp.concatenate([lo_acc, hi_acc], axis=1)
    x2 = x1_ref[...] + gate2_ref[...] * moe
    xf = x2 * lax.rsqrt(jnp.mean(x2 * x2, axis=-1, keepdims=True) + EPS)
    y_ref[...] = xf * gf_ref[...]


def _combine(pos, w, h2p, ys, sg, su, sd, x1, mod3, gf, seq):
    T, D = x1.shape
    assert D // 2 == ROW_TILE * LANES, "a packed token row must fill exactly one (8, 128) tile"
    tt = 256
    per_seq = seq // tt

    def const(shape):
        return pl.BlockSpec(shape, lambda i: (0,) * len(shape))

    return pl.pallas_call(
        functools.partial(_combine_kernel, tt),
        grid=(T // tt,),
        in_specs=[pl.BlockSpec((tt * TOP_K,), lambda i: (i,), memory_space=pltpu.SMEM),
                  pl.BlockSpec((tt, TOP_K), lambda i: (i, 0)),
                  pl.BlockSpec((tt * ROW_TILE, LANES), lambda i: (i, 0)),
                  const(sg.shape), const(su.shape), const(sd.shape),
                  pl.BlockSpec((tt, D), lambda i: (i, 0)),
                  pl.BlockSpec((None, 1, D), lambda i: ((i // per_seq) * 6 + 5, 0, 0)),
                  const((1, D)),
                  pl.BlockSpec(memory_space=pl.ANY)],
        out_specs=pl.BlockSpec((tt, D), lambda i: (i, 0)),
        out_shape=jax.ShapeDtypeStruct((T, D), F32),
        scratch_shapes=[pltpu.VMEM((TOP_K, tt * ROW_TILE, LANES), U32), pltpu.SemaphoreType.DMA],
        compiler_params=_params(("arbitrary",), 48),
        name="combine",
    )(pos, w, h2p, sg, su, sd, x1, mod3, gf, ys)


def kernel(x, c, positions, norm_mix_g, w_ada, b_ada, w_in, conv_w, w_conv_out, w_attn_out,
           w_o, norm_ffn_g, w_router, router_bias, w_exp_gate, w_exp_up, w_exp_down,
           w_sh_gate, w_sh_up, w_sh_down, norm_final_g):
    B, S, D = x.shape
    T = B * S
    assert w_ada.shape[0] == 1, "the final norm is fused into the single layer's combine kernel"
    l = 0
    x2d = x.reshape(T, D)
    mod3 = _ada(c, w_ada[l], b_ada[l]).reshape(B * 6, 1, D)
    h1 = _norm1(x2d, norm_mix_g[l].reshape(1, D), mod3, S)
    proj = _inproj(h1, w_in[l])
    cos2, sins = _rope_tables(positions)
    attn_o = _attention(proj, cos2, sins, B, S)
    x1, h2p, logits_t = _mixer_out(
        proj, attn_o, x2d, conv_w[l], w_conv_out[l].astype(BF16), w_attn_out[l].astype(BF16),
        w_o[l].astype(BF16), mod3, norm_ffn_g[l].reshape(1, D), w_router[l].T, S)
    eid, rank, gate_w, counts = _route(logits_t, router_bias[l])
    offs, *items = _work_items(counts.reshape(N_EXPERTS), T * TOP_K)
    pos = _slots(offs, eid, rank).T.reshape(T * TOP_K)
    xs = _dispatch(pos, h2p)
    ys = _experts(items, xs, w_exp_gate[l], w_exp_up[l], w_exp_down[l])
    y = _combine(pos, gate_w, h2p, ys, w_sh_gate[l].astype(BF16), w_sh_up[l].astype(BF16),
                 w_sh_down[l].astype(BF16), x1, mod3, norm_final_g.reshape(1, D), S)
    return y.reshape(B, S, D)
```

```python
import functools

import jax
import jax.numpy as jnp
from jax import lax
from jax.experimental import pallas as pl
from jax.experimental.pallas import tpu as pltpu

D_MODEL = 2048
HEAD_DIM = 128
ATTN_DILATIONS = (1, 4, 16)
N_GROUPS = 3
HEADS_PER_GROUP = 4
ATTN_WIDTH = N_GROUPS * HEADS_PER_GROUP * HEAD_DIM
ATTN_OUT_WIDTH = HEADS_PER_GROUP * HEAD_DIM
ROPE_THETA = 10000.0
Q_BLOCK = 128
CONV_WIDTH = D_MODEL // 2
CONV_K = 3
IN_COLS = 3 * CONV_WIDTH + 3 * ATTN_WIDTH + 2 * D_MODEL
N_EXPERTS = 64
TOP_K = 8
N_EXPERT_GROUPS = 8
TOP_GROUPS = 4
D_EXPERT = D_MODEL // 4
ROUTED_SCALE = 2.5
EPS = 1e-6
NEG_INF = -1e30

F32 = jnp.float32
BF16 = jnp.bfloat16
U32 = jnp.uint32
MIB = 1024 * 1024

PROJ_BLOCK = 512
N_PROJ_BLOCKS = IN_COLS // PROJ_BLOCK
GATE_SRC_BLOCK = (3 * CONV_WIDTH + 3 * ATTN_WIDTH) // PROJ_BLOCK
N_GATE_BLOCKS = 2 * D_MODEL // PROJ_BLOCK
COL_GCONV = 0
COL_GATTN = D_MODEL
COL_B = 2 * D_MODEL
COL_C = COL_B + CONV_WIDTH
COL_V = COL_C + CONV_WIDTH
COL_Q = COL_V + CONV_WIDTH
COL_K = COL_Q + ATTN_WIDTH
COL_VA = COL_K + ATTN_WIDTH


def _params(semantics, vmem_mib):
    return pltpu.CompilerParams(dimension_semantics=semantics, vmem_limit_bytes=vmem_mib * MIB)


def _sigmoid(x):
    return 1.0 / (1.0 + jnp.exp(-x))


def _pack_halves(x):
    n = x.shape[1] // 2
    return pltpu.pack_elementwise([x[:, :n], x[:, n:]], packed_dtype=BF16)


def _unpack_halves(xp):
    lo = pltpu.unpack_elementwise(xp, index=0, packed_dtype=BF16, unpacked_dtype=F32)
    hi = pltpu.unpack_elementwise(xp, index=1, packed_dtype=BF16, unpacked_dtype=F32)
    return lo, hi


def _unpack_rows_bf16(xp):
    lo, hi = _unpack_halves(xp)
    return jnp.concatenate([lo.astype(BF16), hi.astype(BF16)], axis=1)


ROW_TILE = 8
LANES = 128


def _store_token_tiles(ref, packed):
    n = packed.shape[0]
    for c in range(ROW_TILE):
        ref[pl.ds(c, n, stride=ROW_TILE), :] = packed[:, c * LANES:(c + 1) * LANES]


def _load_token_tiles(ref, n):
    return jnp.concatenate([ref[pl.ds(c, n, stride=ROW_TILE), :] for c in range(ROW_TILE)], axis=1)


def _ada_kernel(c_ref, w_ref, b_ref, o_ref):
    c = c_ref[...]
    s = c * _sigmoid(c)
    o_ref[...] = jnp.dot(s, w_ref[...], precision=lax.Precision.HIGHEST,
                         preferred_element_type=F32) + b_ref[...]


def _ada(c, w_ada, b_ada):
    B, D = c.shape
    N = w_ada.shape[1]
    tn = 1024
    return pl.pallas_call(
        _ada_kernel,
        grid=(N // tn,),
        in_specs=[pl.BlockSpec((B, D), lambda j: (0, 0)),
                  pl.BlockSpec((D, tn), lambda j: (0, j)),
                  pl.BlockSpec((1, tn), lambda j: (0, j))],
        out_specs=pl.BlockSpec((B, tn), lambda j: (0, j)),
        out_shape=jax.ShapeDtypeStruct((B, N), F32),
        compiler_params=_params(("arbitrary",), 40),
        name="ada_mod",
    )(c, w_ada, b_ada.reshape(1, N))


def _modnorm(x, g, scale, shift):
    xf = x * lax.rsqrt(jnp.mean(x * x, axis=-1, keepdims=True) + EPS)
    return (xf * g) * (1.0 + scale) + shift


def _norm1_kernel(x_ref, g_ref, sc_ref, sh_ref, o_ref):
    o_ref[...] = _modnorm(x_ref[...], g_ref[...], sc_ref[...], sh_ref[...]).astype(o_ref.dtype)


def _norm1(x2d, g, mod3, seq):
    T, D = x2d.shape
    tm = 512
    per_seq = seq // tm
    return pl.pallas_call(
        _norm1_kernel,
        grid=(T // tm,),
        in_specs=[pl.BlockSpec((tm, D), lambda i: (i, 0)),
                  pl.BlockSpec((1, D), lambda i: (0, 0)),
                  pl.BlockSpec((None, 1, D), lambda i: ((i // per_seq) * 6 + 1, 0, 0)),
                  pl.BlockSpec((None, 1, D), lambda i: ((i // per_seq) * 6 + 0, 0, 0))],
        out_specs=pl.BlockSpec((tm, D), lambda i: (i, 0)),
        out_shape=jax.ShapeDtypeStruct((T, D), BF16),
        compiler_params=_params(("arbitrary",), 32),
        name="norm_mix",
    )(x2d, g, mod3, mod3)


def _inproj_kernel(h_ref, w_ref, o_ref, wbf_ref):
    @pl.when(pl.program_id(1) == 0)
    def _():
        wbf_ref[...] = w_ref[...].astype(BF16)

    o_ref[...] = jnp.dot(h_ref[...], wbf_ref[...], preferred_element_type=F32).astype(o_ref.dtype)


def _proj_dst_block(j):
    return jnp.where(j < GATE_SRC_BLOCK, j + N_GATE_BLOCKS, j - GATE_SRC_BLOCK)


def _inproj(h, w_in):
    T, D = h.shape
    tm = 2048
    tn = PROJ_BLOCK
    return pl.pallas_call(
        _inproj_kernel,
        grid=(N_PROJ_BLOCKS, T // tm),
        in_specs=[pl.BlockSpec((tm, D), lambda j, i: (i, 0)),
                  pl.BlockSpec((D, tn), lambda j, i: (0, j))],
        out_specs=pl.BlockSpec((tm, tn), lambda j, i: (i, _proj_dst_block(j))),
        out_shape=jax.ShapeDtypeStruct((T, IN_COLS), BF16),
        scratch_shapes=[pltpu.VMEM((D, tn), BF16)],
        compiler_params=_params(("arbitrary", "arbitrary"), 48),
        name="in_proj",
    )(h, w_in)


def _rope_kernel(pos_ref, inv_ref, sign_ref, cos_ref, sin_ref):
    ang = pos_ref[...].astype(F32) * inv_ref[...]
    cos_ref[...] = jnp.cos(ang)
    sin_ref[...] = jnp.sin(ang) * sign_ref[...]


def _rope_tables(positions):
    T = positions.size
    ts = 1024
    inv = ROPE_THETA ** (-jnp.arange(0, HEAD_DIM, 2, dtype=F32) / HEAD_DIM)
    inv2 = jnp.concatenate([inv, inv]).reshape(1, HEAD_DIM)
    sign = jnp.concatenate([-jnp.ones((HEAD_DIM // 2,), F32), jnp.ones((HEAD_DIM // 2,), F32)]).reshape(1, HEAD_DIM)
    return pl.pallas_call(
        _rope_kernel,
        grid=(T // ts,),
        in_specs=[pl.BlockSpec((ts, 1), lambda i: (i, 0)),
                  pl.BlockSpec((1, HEAD_DIM), lambda i: (0, 0)),
                  pl.BlockSpec((1, HEAD_DIM), lambda i: (0, 0))],
        out_specs=[pl.BlockSpec((ts, HEAD_DIM), lambda i: (i, 0)),
                   pl.BlockSpec((ts, HEAD_DIM), lambda i: (i, 0))],
        out_shape=[jax.ShapeDtypeStruct((T, HEAD_DIM), F32)] * 2,
        compiler_params=_params(("arbitrary",), 32),
        name="rope_tables",
    )(positions.reshape(T, 1), inv2, sign)


ROW_CHUNK = 512


def _attn_group(r, seq, q_ref, k_ref, v_ref, cos_ref, sin_ref, nat_ref, qd_ref, kd_ref, vd_ref,
                acc_ref, m_ref, s_ref):
    n_sub = seq // r
    nb = n_sub // Q_BLOCK
    cls = n_sub + Q_BLOCK
    n_chunks = seq // ROW_CHUNK
    scale = HEAD_DIM ** -0.5

    def rope(t_ref, c0, mult):
        t = t_ref[pl.ds(c0, ROW_CHUNK), :].astype(F32)
        cs = cos_ref[pl.ds(c0, ROW_CHUNK), :]
        sn = sin_ref[pl.ds(c0, ROW_CHUNK), :]
        out = t * cs + pltpu.roll(t, HEAD_DIM // 2, 1) * sn
        return out * mult if mult is not None else out

    def deinterleave(dst_ref, dst_stride, dst_off):
        for rho in range(r):
            dst_ref[pl.ds(rho * dst_stride + dst_off, n_sub), :] = (
                nat_ref[pl.ds(rho, n_sub, stride=r), :].astype(BF16))

    zeros_blk = jnp.zeros((Q_BLOCK, HEAD_DIM), BF16)
    for rho in range(r):
        kd_ref[pl.ds(rho * cls, Q_BLOCK), :] = zeros_blk
        vd_ref[pl.ds(rho * cls, Q_BLOCK), :] = zeros_blk

    if r == 1:
        def fill(c, _):
            c0 = pl.multiple_of(c * ROW_CHUNK, ROW_CHUNK)
            qd_ref[pl.ds(c0, ROW_CHUNK), :] = rope(q_ref, c0, scale).astype(BF16)
            kd_ref[pl.ds(Q_BLOCK + c0, ROW_CHUNK), :] = rope(k_ref, c0, None).astype(BF16)
            vd_ref[pl.ds(Q_BLOCK + c0, ROW_CHUNK), :] = v_ref[pl.ds(c0, ROW_CHUNK), :]
            return 0
        lax.fori_loop(0, n_chunks, fill, 0)
    else:
        def fill_q(c, _):
            c0 = pl.multiple_of(c * ROW_CHUNK, ROW_CHUNK)
            nat_ref[pl.ds(c0, ROW_CHUNK), :] = rope(q_ref, c0, scale)
            return 0
        lax.fori_loop(0, n_chunks, fill_q, 0)
        deinterleave(qd_ref, n_sub, 0)

        def fill_k(c, _):
            c0 = pl.multiple_of(c * ROW_CHUNK, ROW_CHUNK)
            nat_ref[pl.ds(c0, ROW_CHUNK), :] = rope(k_ref, c0, None)
            return 0
        lax.fori_loop(0, n_chunks, fill_k, 0)
        deinterleave(kd_ref, cls, Q_BLOCK)

        def fill_v(c, _):
            c0 = pl.multiple_of(c * ROW_CHUNK, ROW_CHUNK)
            nat_ref[pl.ds(c0, ROW_CHUNK), :] = v_ref[pl.ds(c0, ROW_CHUNK), :].astype(F32)
            return 0
        lax.fori_loop(0, n_chunks, fill_v, 0)
        deinterleave(vd_ref, cls, Q_BLOCK)

    qi = lax.broadcasted_iota(jnp.int32, (Q_BLOCK, 2 * Q_BLOCK), 0)
    kj = lax.broadcasted_iota(jnp.int32, (Q_BLOCK, 2 * Q_BLOCK), 1)
    cur_ok = (kj >= Q_BLOCK) & (kj - Q_BLOCK <= qi)
    prev_ok = (kj < Q_BLOCK) & (kj >= qi)

    def block(blk, _):
        rho = blk // nb
        n = blk % nb
        qrow = pl.multiple_of(rho * n_sub + n * Q_BLOCK, Q_BLOCK)
        krow = pl.multiple_of(rho * cls + n * Q_BLOCK, Q_BLOCK)
        q = qd_ref[pl.ds(qrow, Q_BLOCK), :]
        kw = kd_ref[pl.ds(krow, 2 * Q_BLOCK), :]
        vw = vd_ref[pl.ds(krow, 2 * Q_BLOCK), :]
        s = lax.dot_general(q, kw, (((1,), (1,)), ((), ())), preferred_element_type=F32)
        ok = cur_ok | (prev_ok & (n > 0))
        s = jnp.where(ok, s, NEG_INF)
        m = jnp.max(s, axis=1, keepdims=True)
        p = jnp.exp(s - m)
        ssum = jnp.sum(p, axis=1, keepdims=True)
        acc = jnp.dot(p.astype(BF16), vw, preferred_element_type=F32)
        start = rho + n * (Q_BLOCK * r)
        if r == 1:
            rows = pl.ds(pl.multiple_of(start, Q_BLOCK), Q_BLOCK)
        else:
            rows = pl.ds(start, Q_BLOCK, stride=r)
        acc_ref[rows, :] = acc
        m_ref[rows, :] = jnp.broadcast_to(m, (Q_BLOCK, HEAD_DIM))
        s_ref[rows, :] = jnp.broadcast_to(ssum, (Q_BLOCK, HEAD_DIM))
        return 0

    lax.fori_loop(0, r * nb, block, 0, unroll=8)


def _attn_kernel(seq, q_ref, k_ref, v_ref, cos_ref, sin_ref, o_ref,
                 nat_ref, qd_ref, kd_ref, vd_ref, num_ref, den_ref, max_ref, acc_ref, m_ref, s_ref):
    g = pl.program_id(2)
    n_chunks = seq // ROW_CHUNK
    common = (q_ref, k_ref, v_ref, cos_ref, sin_ref, nat_ref, qd_ref, kd_ref, vd_ref)

    @pl.when(g == 0)
    def _():
        _attn_group(ATTN_DILATIONS[0], seq, *common, num_ref, max_ref, den_ref)

    def merge(last):
        def body(c, _):
            rows = pl.ds(pl.multiple_of(c * ROW_CHUNK, ROW_CHUNK), ROW_CHUNK)
            m_old = max_ref[rows, :]
            m_grp = m_ref[rows, :]
            m_new = jnp.maximum(m_old, m_grp)
            a = jnp.exp(m_old - m_new)
            b = jnp.exp(m_grp - m_new)
            num = num_ref[rows, :] * a + acc_ref[rows, :] * b
            den = den_ref[rows, :] * a + s_ref[rows, :] * b
            if last:
                o_ref[rows, :] = (num / den).astype(o_ref.dtype)
            else:
                num_ref[rows, :] = num
                den_ref[rows, :] = den
                max_ref[rows, :] = m_new
            return 0
        lax.fori_loop(0, n_chunks, body, 0)

    for gi in (1, 2):
        @pl.when(g == gi)
        def _(gi=gi):
            _attn_group(ATTN_DILATIONS[gi], seq, *common, acc_ref, m_ref, s_ref)
            merge(last=(gi == N_GROUPS - 1))


def _attention(proj, cos2, sins, batch, seq):
    T = proj.shape[0]
    qb, kb, vb = COL_Q // HEAD_DIM, COL_K // HEAD_DIM, COL_VA // HEAD_DIM
    pad_rows = seq + max(ATTN_DILATIONS) * Q_BLOCK

    def head_spec(base):
        return pl.BlockSpec((seq, HEAD_DIM), lambda b, j, g: (b, base + g * HEADS_PER_GROUP + j))

    tab_spec = pl.BlockSpec((seq, HEAD_DIM), lambda b, j, g: (b, 0))
    big = pltpu.VMEM((seq, HEAD_DIM), F32)
    return pl.pallas_call(
        functools.partial(_attn_kernel, seq),
        grid=(batch, HEADS_PER_GROUP, N_GROUPS),
        in_specs=[head_spec(qb), head_spec(kb), head_spec(vb), tab_spec, tab_spec],
        out_specs=pl.BlockSpec((seq, HEAD_DIM), lambda b, j, g: (b, j)),
        out_shape=jax.ShapeDtypeStruct((T, ATTN_OUT_WIDTH), BF16),
        scratch_shapes=[big,
                        pltpu.VMEM((seq, HEAD_DIM), BF16),
                        pltpu.VMEM((pad_rows, HEAD_DIM), BF16),
                        pltpu.VMEM((pad_rows, HEAD_DIM), BF16),
                        big, big, big, big, big, big],
        compiler_params=_params(("arbitrary", "arbitrary", "arbitrary"), 48),
        name="dilated_attn",
    )(proj, proj, proj, cos2, sins)


def _mix_kernel(seq, ts,
                b_ref, c_ref, v_ref, ch_ref, vh_ref, gc_ref, ga_ref, ao_ref, x_ref,
                cw_ref, wc_ref, wa_ref, wo_ref, gate1_ref, g2_ref, sc2_ref, sh2_ref, wr_ref,
                x1_ref, h2_ref, lt_ref):
    i = pl.program_id(0)
    not_start = ((i * ts) % seq != 0).astype(F32)
    u = c_ref[...].astype(F32) * v_ref[...].astype(F32)
    halo = ch_ref[...].astype(F32) * vh_ref[...].astype(F32) * not_start
    h1 = halo[15:16, :]
    h2 = halo[14:15, :]
    row = lax.broadcasted_iota(jnp.int32, u.shape, 0)
    up1 = jnp.where(row == 0, h1, pltpu.roll(u, 1, 0))
    up2 = jnp.where(row == 0, h2, jnp.where(row == 1, h1, pltpu.roll(u, 2, 0)))
    cw = cw_ref[...]
    conv = cw[0:1, :] * up2 + cw[1:2, :] * up1 + cw[2:3, :] * u
    yc = jnp.dot((b_ref[...].astype(F32) * conv).astype(BF16), wc_ref[...], preferred_element_type=F32)
    ya = jnp.dot(ao_ref[...], wa_ref[...], preferred_element_type=F32)
    merged = _sigmoid(gc_ref[...].astype(F32)) * yc + _sigmoid(ga_ref[...].astype(F32)) * ya
    mo = jnp.dot(merged.astype(BF16), wo_ref[...], preferred_element_type=F32)
    x1 = x_ref[...] + gate1_ref[...] * mo
    x1_ref[...] = x1
    h2n = _modnorm(x1, g2_ref[...], sc2_ref[...], sh2_ref[...])
    _store_token_tiles(h2_ref, _pack_halves(h2n))
    lt_ref[...] = lax.dot_general(wr_ref[...], h2n, (((1,), (1,)), ((), ())),
                                  precision=lax.Precision.HIGHEST, preferred_element_type=F32)


def _mixer_out(proj, attn_o, x2d, conv_w, wc, wa, wo, mod3, g2, wr_t, seq):
    T, D = x2d.shape
    ts = 256
    per_seq = seq // ts
    halo_rows = 16

    def col(width, off):
        return pl.BlockSpec((ts, width), lambda i: (i, off // width))

    def halo(off):
        return pl.BlockSpec((halo_rows, CONV_WIDTH),
                            lambda i: (jnp.maximum(i * (ts // halo_rows) - 1, 0), off // CONV_WIDTH))

    def const(shape):
        return pl.BlockSpec(shape, lambda i: (0,) * len(shape), pipeline_mode=pl.Buffered(1))

    def mod(k):
        return pl.BlockSpec((None, 1, D), lambda i: ((i // per_seq) * 6 + k, 0, 0))

    return pl.pallas_call(
        functools.partial(_mix_kernel, seq, ts),
        grid=(T // ts,),
        in_specs=[col(CONV_WIDTH, COL_B), col(CONV_WIDTH, COL_C), col(CONV_WIDTH, COL_V),
                  halo(COL_C), halo(COL_V),
                  col(D, COL_GCONV), col(D, COL_GATTN),
                  pl.BlockSpec((ts, ATTN_OUT_WIDTH), lambda i: (i, 0)),
                  pl.BlockSpec((ts, D), lambda i: (i, 0)),
                  const((CONV_K, CONV_WIDTH)), const((CONV_WIDTH, D)), const((ATTN_OUT_WIDTH, D)),
                  const((D, D)),
                  mod(2), const((1, D)), mod(4), mod(3), const((N_EXPERTS, D))],
        out_specs=[pl.BlockSpec((ts, D), lambda i: (i, 0)),
                   pl.BlockSpec((ts * ROW_TILE, LANES), lambda i: (i, 0)),
                   pl.BlockSpec((N_EXPERTS, ts), lambda i: (0, i))],
        out_shape=[jax.ShapeDtypeStruct((T, D), F32),
                   jax.ShapeDtypeStruct((T * ROW_TILE, LANES), U32),
                   jax.ShapeDtypeStruct((N_EXPERTS, T), F32)],
        compiler_params=_params(("arbitrary",), 56),
        name="mixer_out",
    )(proj, proj, proj, proj, proj, proj, proj, attn_o, x2d,
      conv_w, wc, wa, wo, mod3, g2, mod3, mod3, wr_t)


def _route_kernel(lt_ref, bias_ref, eid_ref, rank_ref, w_ref, cnt_ref, carry_ref):
    @pl.when(pl.program_id(0) == 0)
    def _():
        carry_ref[...] = jnp.zeros_like(carry_ref)

    lt = lt_ref[...]
    tr = lt.shape[1]
    per = N_EXPERTS // N_EXPERT_GROUPS
    scores = _sigmoid(lt)
    sel = scores + bias_ref[...]
    sel3 = sel.reshape(N_EXPERT_GROUPS, per, tr)
    sub = lax.broadcasted_iota(jnp.int32, sel3.shape, 1)
    top1 = jnp.max(sel3, axis=1, keepdims=True)
    first = jnp.min(jnp.where(sel3 == top1, sub, per), axis=1, keepdims=True)
    top2 = jnp.max(jnp.where(sub == first, -jnp.inf, sel3), axis=1, keepdims=True)
    gs = (top1 + top2).reshape(N_EXPERT_GROUPS, tr)
    gidx = lax.broadcasted_iota(jnp.int32, gs.shape, 0)
    grank = jnp.zeros(gs.shape, jnp.int32)
    for other in range(N_EXPERT_GROUPS):
        row = gs[other:other + 1, :]
        grank += ((row > gs) | ((row == gs) & (other < gidx))).astype(jnp.int32)
    gmask = grank < TOP_GROUPS
    emask = jnp.broadcast_to(gmask[:, None, :], sel3.shape).reshape(N_EXPERTS, tr)
    cand = jnp.where(emask, sel, NEG_INF)
    eidx = lax.broadcasted_iota(jnp.int32, cand.shape, 0)
    hits, eids, ws = [], [], []
    for _ in range(TOP_K):
        mx = jnp.max(cand, axis=0, keepdims=True)
        idx = jnp.min(jnp.where(cand == mx, eidx, N_EXPERTS), axis=0, keepdims=True)
        hit = eidx == idx
        hits.append(hit)
        eids.append(idx)
        ws.append(jnp.sum(jnp.where(hit, scores, 0.0), axis=0, keepdims=True))
        cand = jnp.where(hit, -jnp.inf, cand)
    chosen = hits[0]
    for hit in hits[1:]:
        chosen = chosen | hit
    chosen = chosen.astype(F32)
    before = (lax.broadcasted_iota(jnp.int32, (tr, tr), 0) < lax.broadcasted_iota(jnp.int32, (tr, tr), 1))
    seen = jnp.dot(chosen.astype(BF16), before.astype(BF16), preferred_element_type=F32) + carry_ref[...]
    ranks = [jnp.sum(jnp.where(hit, seen, 0.0), axis=0, keepdims=True) for hit in hits]
    carry_ref[...] += jnp.sum(chosen, axis=1, keepdims=True)
    cnt_ref[...] = carry_ref[...].astype(jnp.int32)
    wsum = ws[0]
    for w in ws[1:]:
        wsum = wsum + w
    w8 = jnp.concatenate(ws, axis=0) / wsum * ROUTED_SCALE
    w_ref[...] = w8.T
    eid_ref[...] = jnp.concatenate(eids, axis=0)
    rank_ref[...] = jnp.concatenate(ranks, axis=0).astype(jnp.int32)


def _route(logits_t, router_bias):
    E, T = logits_t.shape
    tr = 512
    return pl.pallas_call(
        _route_kernel,
        grid=(T // tr,),
        in_specs=[pl.BlockSpec((E, tr), lambda i: (0, i)),
                  pl.BlockSpec((E, 1), lambda i: (0, 0))],
        out_specs=[pl.BlockSpec((TOP_K, tr), lambda i: (0, i)),
                   pl.BlockSpec((TOP_K, tr), lambda i: (0, i)),
                   pl.BlockSpec((tr, TOP_K), lambda i: (i, 0)),
                   pl.BlockSpec((E, 1), lambda i: (0, 0))],
        out_shape=[jax.ShapeDtypeStruct((TOP_K, T), jnp.int32),
                   jax.ShapeDtypeStruct((TOP_K, T), jnp.int32),
                   jax.ShapeDtypeStruct((T, TOP_K), F32),
                   jax.ShapeDtypeStruct((E, 1), jnp.int32)],
        scratch_shapes=[pltpu.VMEM((E, 1), F32)],
        compiler_params=_params(("arbitrary",), 32),
        name="router",
    )(logits_t, router_bias.reshape(E, 1))


def _slot_kernel(offs_ref, eid_ref, rank_ref, pos_ref):
    eid = eid_ref[...]
    pos = rank_ref[...]
    for e in range(N_EXPERTS):
        pos = pos + jnp.where(eid == e, offs_ref[e], 0)
    pos_ref[...] = pos * ROW_TILE


def _slots(offs, eid, rank):
    return pl.pallas_call(
        _slot_kernel,
        in_specs=[pl.BlockSpec(memory_space=pltpu.SMEM),
                  pl.BlockSpec(memory_space=pltpu.VMEM),
                  pl.BlockSpec(memory_space=pltpu.VMEM)],
        out_specs=pl.BlockSpec(memory_space=pltpu.VMEM),
        out_shape=jax.ShapeDtypeStruct(eid.shape, jnp.int32),
        name="slot_rows",
    )(offs, eid, rank)


def _tile_at(ref, row):
    return ref.at[pl.ds(pl.multiple_of(row, ROW_TILE), ROW_TILE)]


def _dispatch_kernel(tt, pos_ref, h_ref, sg_ref, su_ref, sd_ref, xs_ref, shared_ref, sem):
    def body(t, _):
        src = _tile_at(h_ref, t * ROW_TILE)
        for k in range(TOP_K):
            pltpu.make_async_copy(src, _tile_at(xs_ref, pos_ref[t * TOP_K + k]), sem).start(priority=k % 2)
        return 0
    lax.fori_loop(0, tt, body, 0)

    x = _unpack_rows_bf16(_load_token_tiles(h_ref, tt))
    a = jnp.dot(x, sg_ref[...], preferred_element_type=F32)
    u = jnp.dot(x, su_ref[...], preferred_element_type=F32)
    shared_ref[...] = jnp.dot(((a * _sigmoid(a)) * u).astype(BF16), sd_ref[...], preferred_element_type=F32)

    for k in range(TOP_K):
        pltpu.make_async_copy(h_ref, xs_ref.at[pl.ds(0, tt * ROW_TILE)], sem).wait()


def _dispatch(pos, h2p, sg, su, sd):
    T = h2p.shape[0] // ROW_TILE
    D = sg.shape[0]
    tt = 512

    def const(shape):
        return pl.BlockSpec(shape, lambda i: (0,) * len(shape))

    return pl.pallas_call(
        functools.partial(_dispatch_kernel, tt),
        grid=(T // tt,),
        in_specs=[pl.BlockSpec((tt * TOP_K,), lambda i: (i,), memory_space=pltpu.SMEM),
                  pl.BlockSpec((tt * ROW_TILE, LANES), lambda i: (i, 0)),
                  const(sg.shape), const(su.shape), const(sd.shape)],
        out_specs=[pl.BlockSpec(memory_space=pl.ANY),
                   pl.BlockSpec((tt, D), lambda i: (i, 0))],
        out_shape=[jax.ShapeDtypeStruct((T * TOP_K * ROW_TILE, LANES), U32),
                   jax.ShapeDtypeStruct((T, D), F32)],
        scratch_shapes=[pltpu.SemaphoreType.DMA],
        compiler_params=_params(("arbitrary",), 48),
        name="dispatch",
    )(pos, h2p, sg, su, sd)


EXPERT_TILE = 256


def _work_items(counts, n_rows):
    tm = EXPERT_TILE
    n_tiles = n_rows // tm
    ends = jnp.cumsum(counts)
    offs = ends - counts
    cuts = jnp.sort(jnp.concatenate([jnp.arange(n_tiles, dtype=jnp.int32) * tm, offs]))
    nxt = jnp.concatenate([cuts[1:], jnp.array([n_rows], jnp.int32)])
    tile = jnp.minimum(cuts // tm, n_tiles - 1)
    expert = jnp.minimum(jnp.sum((ends[None, :] <= cuts[:, None]).astype(jnp.int32), axis=1), N_EXPERTS - 1)
    lo = cuts - tile * tm
    hi = nxt - tile * tm
    fresh = jnp.concatenate([jnp.ones((1,), jnp.int32), (expert[1:] != expert[:-1]).astype(jnp.int32)])
    n = cuts.shape[0]
    slot = (jnp.cumsum(fresh) - 1) % 2
    fresh_at = jnp.where(fresh == 1, jnp.arange(n, dtype=jnp.int32), n)
    later = jnp.concatenate([lax.cummin(fresh_at, axis=0, reverse=True)[1:], jnp.array([n], jnp.int32)])
    upcoming = jnp.where(later < n, expert[jnp.minimum(later, n - 1)], -1)
    return offs, tile, expert, lo, hi, fresh, slot, upcoming


def _experts_kernel(tile_ref, exp_ref, lo_ref, hi_ref, fresh_ref, slot_ref, upcoming_ref,
                    xs_ref, wg_hbm, wu_hbm, wd_hbm, ys_ref,
                    wg_st, wu_st, wd_st, wgb_ref, wub_ref, wdb_ref, acc_ref, sems):
    i = pl.program_id(0)
    lo = lo_ref[i]
    hi = hi_ref[i]

    def weight_copies(e, s):
        return (pltpu.make_async_copy(wg_hbm.at[e], wg_st.at[s], sems.at[s, 0]),
                pltpu.make_async_copy(wu_hbm.at[e], wu_st.at[s], sems.at[s, 1]),
                pltpu.make_async_copy(wd_hbm.at[e], wd_st.at[s], sems.at[s, 2]))

    @pl.when(fresh_ref[i] == 1)
    def _():
        s = slot_ref[i]

        @pl.when(i == 0)
        def _():
            for cp in weight_copies(exp_ref[i], s):
                cp.start()

        for cp in weight_copies(exp_ref[i], s):
            cp.wait()
        for static_s in range(2):
            @pl.when(s == static_s)
            def _(static_s=static_s):
                wgb_ref[...] = wg_st[static_s].astype(BF16)
                wub_ref[...] = wu_st[static_s].astype(BF16)
                wdb_ref[...] = wd_st[static_s].astype(BF16)

        @pl.when(upcoming_ref[i] >= 0)
        def _():
            for cp in weight_copies(upcoming_ref[i], 1 - s):
                cp.start()

    def piece():
        x = _unpack_rows_bf16(_load_token_tiles(xs_ref, acc_ref.shape[0]))
        a = jnp.dot(x, wgb_ref[...], preferred_element_type=F32)
        u = jnp.dot(x, wub_ref[...], preferred_element_type=F32)
        row = lax.broadcasted_iota(jnp.int32, (a.shape[0], 1), 0)
        h = jnp.where((row >= lo) & (row < hi), (a * _sigmoid(a)) * u, 0.0)
        return jnp.dot(h.astype(BF16), wdb_ref[...], preferred_element_type=F32)

    @pl.when((hi > lo) & (lo == 0))
    def _():
        acc_ref[...] = piece()

    @pl.when((hi > lo) & (lo > 0))
    def _():
        acc_ref[...] += piece()

    @pl.when((hi <= lo) & (lo == 0))
    def _():
        acc_ref[...] = jnp.zeros_like(acc_ref)

    _store_token_tiles(ys_ref, _pack_halves(acc_ref[...]))


def _experts(items, xs, wg, wu, wd):
    n_rows = xs.shape[0] // ROW_TILE
    _, D, F = wg.shape
    tm = EXPERT_TILE
    n_items = items[0].shape[0]
    tiles = pl.BlockSpec((tm * ROW_TILE, LANES), lambda i, tile, *_: (tile[i], 0))
    hbm = pl.BlockSpec(memory_space=pl.ANY)
    grid_spec = pltpu.PrefetchScalarGridSpec(
        num_scalar_prefetch=len(items),
        grid=(n_items,),
        in_specs=[tiles, hbm, hbm, hbm],
        out_specs=tiles,
        scratch_shapes=[pltpu.VMEM((2, D, F), F32), pltpu.VMEM((2, D, F), F32), pltpu.VMEM((2, F, D), F32),
                        pltpu.VMEM((D, F), BF16), pltpu.VMEM((D, F), BF16), pltpu.VMEM((F, D), BF16),
                        pltpu.VMEM((tm, D), F32), pltpu.SemaphoreType.DMA((2, 3))])
    return pl.pallas_call(
        _experts_kernel,
        grid_spec=grid_spec,
        out_shape=jax.ShapeDtypeStruct((n_rows * ROW_TILE, LANES), U32),
        compiler_params=_params(("arbitrary",), 56),
        name="routed_experts",
    )(*items, xs, wg, wu, wd)


def _combine_kernel(tt, pos_ref, w_ref, shared_ref, x1_ref, gate2_ref, gf_ref,
                    ys_ref, y_ref, buf_ref, sem):
    def body(t, _):
        for k in range(TOP_K):
            pltpu.make_async_copy(_tile_at(ys_ref, pos_ref[t * TOP_K + k]), _tile_at(buf_ref.at[k], t * ROW_TILE),
                                  sem).start(priority=k % 2)
        return 0
    lax.fori_loop(0, tt, body, 0)

    for k in range(TOP_K):
        pltpu.make_async_copy(ys_ref.at[pl.ds(0, tt * ROW_TILE)], buf_ref.at[k], sem).wait()
    moe = shared_ref[...]
    half = moe.shape[1] // 2
    w = w_ref[...]
    lo_acc = moe[:, :half]
    hi_acc = moe[:, half:]
    for k in range(TOP_K):
        lo, hi = _unpack_halves(_load_token_tiles(buf_ref.at[k], tt))
        wk = w[:, k:k + 1]
        lo_acc = lo_acc + wk * lo
        hi_acc = hi_acc + wk * hi
    moe = jnp.concatenate([lo_acc, hi_acc], axis=1)
    x2 = x1_ref[...] + gate2_ref[...] * moe
    xf = x2 * lax.rsqrt(jnp.mean(x2 * x2, axis=-1, keepdims=True) + EPS)
    y_ref[...] = xf * gf_ref[...]


def _combine(pos, w, shared, ys, x1, mod3, gf, seq):
    T, D = x1.shape
    assert D // 2 == ROW_TILE * LANES, "a packed token row must fill exactly one (8, 128) tile"
    tt = 256
    per_seq = seq // tt

    def const(shape):
        return pl.BlockSpec(shape, lambda i: (0,) * len(shape))

    return pl.pallas_call(
        functools.partial(_combine_kernel, tt),
        grid=(T // tt,),
        in_specs=[pl.BlockSpec((tt * TOP_K,), lambda i: (i,), memory_space=pltpu.SMEM),
                  pl.BlockSpec((tt, TOP_K), lambda i: (i, 0)),
                  pl.BlockSpec((tt, D), lambda i: (i, 0)),
                  pl.BlockSpec((tt, D), lambda i: (i, 0)),
                  pl.BlockSpec((None, 1, D), lambda i: ((i // per_seq) * 6 + 5, 0, 0)),
                  const((1, D)),
                  pl.BlockSpec(memory_space=pl.ANY)],
        out_specs=pl.BlockSpec((tt, D), lambda i: (i, 0)),
        out_shape=jax.ShapeDtypeStruct((T, D), F32),
        scratch_shapes=[pltpu.VMEM((TOP_K, tt * ROW_TILE, LANES), U32), pltpu.SemaphoreType.DMA],
        compiler_params=_params(("arbitrary",), 48),
        name="combine",
    )(pos, w, shared, x1, mod3, gf, ys)


def kernel(x, c, positions, norm_mix_g, w_ada, b_ada, w_in, conv_w, w_conv_out, w_attn_out,
           w_o, norm_ffn_g, w_router, router_bias, w_exp_gate, w_exp_up, w_exp_down,
           w_sh_gate, w_sh_up, w_sh_down, norm_final_g):
    B, S, D = x.shape
    T = B * S
    assert w_ada.shape[0] == 1, "the final norm is fused into the single layer's combine kernel"
    l = 0
    x2d = x.reshape(T, D)
    mod3 = _ada(c, w_ada[l], b_ada[l]).reshape(B * 6, 1, D)
    h1 = _norm1(x2d, norm_mix_g[l].reshape(1, D), mod3, S)
    proj = _inproj(h1, w_in[l])
    cos2, sins = _rope_tables(positions)
    attn_o = _attention(proj, cos2, sins, B, S)
    x1, h2p, logits_t = _mixer_out(
        proj, attn_o, x2d, conv_w[l], w_conv_out[l].astype(BF16), w_attn_out[l].astype(BF16),
        w_o[l].astype(BF16), mod3, norm_ffn_g[l].reshape(1, D), w_router[l].T, S)
    eid, rank, gate_w, counts = _route(logits_t, router_bias[l])
    offs, *items = _work_items(counts.reshape(N_EXPERTS), T * TOP_K)
    pos = _slots(offs, eid, rank).T.reshape(T * TOP_K)
    xs, shared = _dispatch(pos, h2p, w_sh_gate[l].astype(BF16), w_sh_up[l].astype(BF16),
                           w_sh_down[l].astype(BF16))
    ys = _experts(items, xs, w_exp_gate[l], w_exp_up[l], w_exp_down[l])
    y = _combine(pos, gate_w, shared, ys, x1, mod3, norm_final_g.reshape(1, D), S)
    return y.reshape(B, S, D)
```

```python
import functools

import jax
import jax.numpy as jnp
from jax import lax
from jax.experimental import pallas as pl
from jax.experimental.pallas import tpu as pltpu

D_MODEL = 2048
HEAD_DIM = 128
ATTN_DILATIONS = (1, 4, 16)
N_GROUPS = 3
HEADS_PER_GROUP = 4
ATTN_WIDTH = N_GROUPS * HEADS_PER_GROUP * HEAD_DIM
ATTN_OUT_WIDTH = HEADS_PER_GROUP * HEAD_DIM
ROPE_THETA = 10000.0
Q_BLOCK = 128
CONV_WIDTH = D_MODEL // 2
CONV_K = 3
IN_COLS = 3 * CONV_WIDTH + 3 * ATTN_WIDTH + 2 * D_MODEL
N_EXPERTS = 64
TOP_K = 8
N_EXPERT_GROUPS = 8
TOP_GROUPS = 4
D_EXPERT = D_MODEL // 4
ROUTED_SCALE = 2.5
EPS = 1e-6
NEG_INF = -1e30

F32 = jnp.float32
BF16 = jnp.bfloat16
U32 = jnp.uint32
MIB = 1024 * 1024

PROJ_BLOCK = 512
N_PROJ_BLOCKS = IN_COLS // PROJ_BLOCK
GATE_SRC_BLOCK = (3 * CONV_WIDTH + 3 * ATTN_WIDTH) // PROJ_BLOCK
N_GATE_BLOCKS = 2 * D_MODEL // PROJ_BLOCK
COL_GCONV = 0
COL_GATTN = D_MODEL
COL_B = 2 * D_MODEL
COL_C = COL_B + CONV_WIDTH
COL_V = COL_C + CONV_WIDTH
COL_Q = COL_V + CONV_WIDTH
COL_K = COL_Q + ATTN_WIDTH
COL_VA = COL_K + ATTN_WIDTH


def _params(semantics, vmem_mib):
    return pltpu.CompilerParams(dimension_semantics=semantics, vmem_limit_bytes=vmem_mib * MIB)


def _sigmoid(x):
    return 1.0 / (1.0 + jnp.exp(-x))


def _pack_halves(x):
    n = x.shape[1] // 2
    return pltpu.pack_elementwise([x[:, :n], x[:, n:]], packed_dtype=BF16)


def _unpack_halves(xp):
    lo = pltpu.unpack_elementwise(xp, index=0, packed_dtype=BF16, unpacked_dtype=F32)
    hi = pltpu.unpack_elementwise(xp, index=1, packed_dtype=BF16, unpacked_dtype=F32)
    return lo, hi


def _unpack_rows_bf16(xp):
    lo, hi = _unpack_halves(xp)
    return jnp.concatenate([lo.astype(BF16), hi.astype(BF16)], axis=1)


ROW_TILE = 8
LANES = 128


def _store_token_tiles(ref, packed):
    n = packed.shape[0]
    for c in range(ROW_TILE):
        ref[pl.ds(c, n, stride=ROW_TILE), :] = packed[:, c * LANES:(c + 1) * LANES]


def _load_token_tiles(ref, n):
    return jnp.concatenate([ref[pl.ds(c, n, stride=ROW_TILE), :] for c in range(ROW_TILE)], axis=1)


def _ada_kernel(c_ref, w_ref, b_ref, o_ref):
    c = c_ref[...]
    s = c * _sigmoid(c)
    o_ref[...] = jnp.dot(s, w_ref[...], precision=lax.Precision.HIGHEST,
                         preferred_element_type=F32) + b_ref[...]


def _ada(c, w_ada, b_ada):
    B, D = c.shape
    N = w_ada.shape[1]
    tn = 1024
    return pl.pallas_call(
        _ada_kernel,
        grid=(N // tn,),
        in_specs=[pl.BlockSpec((B, D), lambda j: (0, 0)),
                  pl.BlockSpec((D, tn), lambda j: (0, j)),
                  pl.BlockSpec((1, tn), lambda j: (0, j))],
        out_specs=pl.BlockSpec((B, tn), lambda j: (0, j)),
        out_shape=jax.ShapeDtypeStruct((B, N), F32),
        compiler_params=_params(("arbitrary",), 40),
        name="ada_mod",
    )(c, w_ada, b_ada.reshape(1, N))


def _modnorm(x, g, scale, shift):
    xf = x * lax.rsqrt(jnp.mean(x * x, axis=-1, keepdims=True) + EPS)
    return (xf * g) * (1.0 + scale) + shift


def _norm1_kernel(x_ref, g_ref, sc_ref, sh_ref, o_ref):
    o_ref[...] = _modnorm(x_ref[...], g_ref[...], sc_ref[...], sh_ref[...]).astype(o_ref.dtype)


def _norm1(x2d, g, mod3, seq):
    T, D = x2d.shape
    tm = 512
    per_seq = seq // tm
    return pl.pallas_call(
        _norm1_kernel,
        grid=(T // tm,),
        in_specs=[pl.BlockSpec((tm, D), lambda i: (i, 0)),
                  pl.BlockSpec((1, D), lambda i: (0, 0)),
                  pl.BlockSpec((None, 1, D), lambda i: ((i // per_seq) * 6 + 1, 0, 0)),
                  pl.BlockSpec((None, 1, D), lambda i: ((i // per_seq) * 6 + 0, 0, 0))],
        out_specs=pl.BlockSpec((tm, D), lambda i: (i, 0)),
        out_shape=jax.ShapeDtypeStruct((T, D), BF16),
        compiler_params=_params(("arbitrary",), 32),
        name="norm_mix",
    )(x2d, g, mod3, mod3)


def _inproj_kernel(h_ref, w_ref, o_ref, wbf_ref):
    @pl.when(pl.program_id(1) == 0)
    def _():
        wbf_ref[...] = w_ref[...].astype(BF16)

    o_ref[...] = jnp.dot(h_ref[...], wbf_ref[...], preferred_element_type=F32).astype(o_ref.dtype)


def _proj_dst_block(j):
    return jnp.where(j < GATE_SRC_BLOCK, j + N_GATE_BLOCKS, j - GATE_SRC_BLOCK)


def _inproj(h, w_in):
    T, D = h.shape
    tm = 2048
    tn = PROJ_BLOCK
    return pl.pallas_call(
        _inproj_kernel,
        grid=(N_PROJ_BLOCKS, T // tm),
        in_specs=[pl.BlockSpec((tm, D), lambda j, i: (i, 0)),
                  pl.BlockSpec((D, tn), lambda j, i: (0, j))],
        out_specs=pl.BlockSpec((tm, tn), lambda j, i: (i, _proj_dst_block(j))),
        out_shape=jax.ShapeDtypeStruct((T, IN_COLS), BF16),
        scratch_shapes=[pltpu.VMEM((D, tn), BF16)],
        compiler_params=_params(("arbitrary", "arbitrary"), 48),
        name="in_proj",
    )(h, w_in)


def _rope_kernel(pos_ref, inv_ref, sign_ref, cos_ref, sin_ref):
    ang = pos_ref[...].astype(F32) * inv_ref[...]
    cos_ref[...] = jnp.cos(ang)
    sin_ref[...] = jnp.sin(ang) * sign_ref[...]


def _rope_tables(positions):
    T = positions.size
    ts = 1024
    inv = ROPE_THETA ** (-jnp.arange(0, HEAD_DIM, 2, dtype=F32) / HEAD_DIM)
    inv2 = jnp.concatenate([inv, inv]).reshape(1, HEAD_DIM)
    sign = jnp.concatenate([-jnp.ones((HEAD_DIM // 2,), F32), jnp.ones((HEAD_DIM // 2,), F32)]).reshape(1, HEAD_DIM)
    return pl.pallas_call(
        _rope_kernel,
        grid=(T // ts,),
        in_specs=[pl.BlockSpec((ts, 1), lambda i: (i, 0)),
                  pl.BlockSpec((1, HEAD_DIM), lambda i: (0, 0)),
                  pl.BlockSpec((1, HEAD_DIM), lambda i: (0, 0))],
        out_specs=[pl.BlockSpec((ts, HEAD_DIM), lambda i: (i, 0)),
                   pl.BlockSpec((ts, HEAD_DIM), lambda i: (i, 0))],
        out_shape=[jax.ShapeDtypeStruct((T, HEAD_DIM), F32)] * 2,
        compiler_params=_params(("arbitrary",), 32),
        name="rope_tables",
    )(positions.reshape(T, 1), inv2, sign)


ROW_CHUNK = 512


def _attn_group(r, seq, q_ref, k_ref, v_ref, cos_ref, sin_ref, nat_ref, qd_ref, kd_ref, vd_ref,
                acc_ref, m_ref, s_ref):
    n_sub = seq // r
    nb = n_sub // Q_BLOCK
    cls = n_sub + Q_BLOCK
    n_chunks = seq // ROW_CHUNK
    scale = HEAD_DIM ** -0.5

    def rope(t_ref, c0, mult):
        t = t_ref[pl.ds(c0, ROW_CHUNK), :].astype(F32)
        cs = cos_ref[pl.ds(c0, ROW_CHUNK), :]
        sn = sin_ref[pl.ds(c0, ROW_CHUNK), :]
        out = t * cs + pltpu.roll(t, HEAD_DIM // 2, 1) * sn
        return out * mult if mult is not None else out

    def deinterleave(dst_ref, dst_stride, dst_off):
        for rho in range(r):
            dst_ref[pl.ds(rho * dst_stride + dst_off, n_sub), :] = (
                nat_ref[pl.ds(rho, n_sub, stride=r), :].astype(BF16))

    zeros_blk = jnp.zeros((Q_BLOCK, HEAD_DIM), BF16)
    for rho in range(r):
        kd_ref[pl.ds(rho * cls, Q_BLOCK), :] = zeros_blk
        vd_ref[pl.ds(rho * cls, Q_BLOCK), :] = zeros_blk

    if r == 1:
        def fill(c, _):
            c0 = pl.multiple_of(c * ROW_CHUNK, ROW_CHUNK)
            qd_ref[pl.ds(c0, ROW_CHUNK), :] = rope(q_ref, c0, scale).astype(BF16)
            kd_ref[pl.ds(Q_BLOCK + c0, ROW_CHUNK), :] = rope(k_ref, c0, None).astype(BF16)
            vd_ref[pl.ds(Q_BLOCK + c0, ROW_CHUNK), :] = v_ref[pl.ds(c0, ROW_CHUNK), :]
            return 0
        lax.fori_loop(0, n_chunks, fill, 0)
    else:
        def fill_q(c, _):
            c0 = pl.multiple_of(c * ROW_CHUNK, ROW_CHUNK)
            nat_ref[pl.ds(c0, ROW_CHUNK), :] = rope(q_ref, c0, scale)
            return 0
        lax.fori_loop(0, n_chunks, fill_q, 0)
        deinterleave(qd_ref, n_sub, 0)

        def fill_k(c, _):
            c0 = pl.multiple_of(c * ROW_CHUNK, ROW_CHUNK)
            nat_ref[pl.ds(c0, ROW_CHUNK), :] = rope(k_ref, c0, None)
            return 0
        lax.fori_loop(0, n_chunks, fill_k, 0)
        deinterleave(kd_ref, cls, Q_BLOCK)

        def fill_v(c, _):
            c0 = pl.multiple_of(c * ROW_CHUNK, ROW_CHUNK)
            nat_ref[pl.ds(c0, ROW_CHUNK), :] = v_ref[pl.ds(c0, ROW_CHUNK), :].astype(F32)
            return 0
        lax.fori_loop(0, n_chunks, fill_v, 0)
        deinterleave(vd_ref, cls, Q_BLOCK)

    qi = lax.broadcasted_iota(jnp.int32, (Q_BLOCK, 2 * Q_BLOCK), 0)
    kj = lax.broadcasted_iota(jnp.int32, (Q_BLOCK, 2 * Q_BLOCK), 1)
    cur_ok = (kj >= Q_BLOCK) & (kj - Q_BLOCK <= qi)
    prev_ok = (kj < Q_BLOCK) & (kj >= qi)

    def block(blk, _):
        rho = blk // nb
        n = blk % nb
        qrow = pl.multiple_of(rho * n_sub + n * Q_BLOCK, Q_BLOCK)
        krow = pl.multiple_of(rho * cls + n * Q_BLOCK, Q_BLOCK)
        q = qd_ref[pl.ds(qrow, Q_BLOCK), :]
        kw = kd_ref[pl.ds(krow, 2 * Q_BLOCK), :]
        vw = vd_ref[pl.ds(krow, 2 * Q_BLOCK), :]
        s = lax.dot_general(q, kw, (((1,), (1,)), ((), ())), preferred_element_type=F32)
        ok = cur_ok | (prev_ok & (n > 0))
        s = jnp.where(ok, s, NEG_INF)
        m = jnp.max(s, axis=1, keepdims=True)
        p = jnp.exp(s - m)
        ssum = jnp.sum(p, axis=1, keepdims=True)
        acc = jnp.dot(p.astype(BF16), vw, preferred_element_type=F32)
        start = rho + n * (Q_BLOCK * r)
        if r == 1:
            rows = pl.ds(pl.multiple_of(start, Q_BLOCK), Q_BLOCK)
        else:
            rows = pl.ds(start, Q_BLOCK, stride=r)
        acc_ref[rows, :] = acc
        m_ref[rows, :] = jnp.broadcast_to(m, (Q_BLOCK, HEAD_DIM))
        s_ref[rows, :] = jnp.broadcast_to(ssum, (Q_BLOCK, HEAD_DIM))
        return 0

    lax.fori_loop(0, r * nb, block, 0, unroll=8)


def _attn_kernel(seq, q_ref, k_ref, v_ref, cos_ref, sin_ref, o_ref,
                 nat_ref, qd_ref, kd_ref, vd_ref, num_ref, den_ref, max_ref, acc_ref, m_ref, s_ref):
    g = pl.program_id(2)
    n_chunks = seq // ROW_CHUNK
    common = (q_ref, k_ref, v_ref, cos_ref, sin_ref, nat_ref, qd_ref, kd_ref, vd_ref)

    @pl.when(g == 0)
    def _():
        _attn_group(ATTN_DILATIONS[0], seq, *common, num_ref, max_ref, den_ref)

    def merge(last):
        def body(c, _):
            rows = pl.ds(pl.multiple_of(c * ROW_CHUNK, ROW_CHUNK), ROW_CHUNK)
            m_old = max_ref[rows, :]
            m_grp = m_ref[rows, :]
            m_new = jnp.maximum(m_old, m_grp)
            a = jnp.exp(m_old - m_new)
            b = jnp.exp(m_grp - m_new)
            num = num_ref[rows, :] * a + acc_ref[rows, :] * b
            den = den_ref[rows, :] * a + s_ref[rows, :] * b
            if last:
                o_ref[rows, :] = (num / den).astype(o_ref.dtype)
            else:
                num_ref[rows, :] = num
                den_ref[rows, :] = den
                max_ref[rows, :] = m_new
            return 0
        lax.fori_loop(0, n_chunks, body, 0)

    for gi in (1, 2):
        @pl.when(g == gi)
        def _(gi=gi):
            _attn_group(ATTN_DILATIONS[gi], seq, *common, acc_ref, m_ref, s_ref)
            merge(last=(gi == N_GROUPS - 1))


def _attention(proj, cos2, sins, batch, seq):
    T = proj.shape[0]
    qb, kb, vb = COL_Q // HEAD_DIM, COL_K // HEAD_DIM, COL_VA // HEAD_DIM
    pad_rows = seq + max(ATTN_DILATIONS) * Q_BLOCK

    def head_spec(base):
        return pl.BlockSpec((seq, HEAD_DIM), lambda b, j, g: (b, base + g * HEADS_PER_GROUP + j))

    tab_spec = pl.BlockSpec((seq, HEAD_DIM), lambda b, j, g: (b, 0))
    big = pltpu.VMEM((seq, HEAD_DIM), F32)
    return pl.pallas_call(
        functools.partial(_attn_kernel, seq),
        grid=(batch, HEADS_PER_GROUP, N_GROUPS),
        in_specs=[head_spec(qb), head_spec(kb), head_spec(vb), tab_spec, tab_spec],
        out_specs=pl.BlockSpec((seq, HEAD_DIM), lambda b, j, g: (b, j)),
        out_shape=jax.ShapeDtypeStruct((T, ATTN_OUT_WIDTH), BF16),
        scratch_shapes=[big,
                        pltpu.VMEM((seq, HEAD_DIM), BF16),
                        pltpu.VMEM((pad_rows, HEAD_DIM), BF16),
                        pltpu.VMEM((pad_rows, HEAD_DIM), BF16),
                        big, big, big, big, big, big],
        compiler_params=_params(("arbitrary", "arbitrary", "arbitrary"), 48),
        name="dilated_attn",
    )(proj, proj, proj, cos2, sins)


def _mix_kernel(seq, ts,
                b_ref, c_ref, v_ref, ch_ref, vh_ref, gc_ref, ga_ref, ao_ref, x_ref,
                cw_ref, wc_ref, wa_ref, wo_ref, gate1_ref, g2_ref, sc2_ref, sh2_ref, wr_ref,
                x1_ref, h2_ref, lt_ref):
    i = pl.program_id(0)
    not_start = ((i * ts) % seq != 0).astype(F32)
    u = c_ref[...].astype(F32) * v_ref[...].astype(F32)
    halo = ch_ref[...].astype(F32) * vh_ref[...].astype(F32) * not_start
    h1 = halo[15:16, :]
    h2 = halo[14:15, :]
    row = lax.broadcasted_iota(jnp.int32, u.shape, 0)
    up1 = jnp.where(row == 0, h1, pltpu.roll(u, 1, 0))
    up2 = jnp.where(row == 0, h2, jnp.where(row == 1, h1, pltpu.roll(u, 2, 0)))
    cw = cw_ref[...]
    conv = cw[0:1, :] * up2 + cw[1:2, :] * up1 + cw[2:3, :] * u
    yc = jnp.dot((b_ref[...].astype(F32) * conv).astype(BF16), wc_ref[...], preferred_element_type=F32)
    ya = jnp.dot(ao_ref[...], wa_ref[...], preferred_element_type=F32)
    merged = _sigmoid(gc_ref[...].astype(F32)) * yc + _sigmoid(ga_ref[...].astype(F32)) * ya
    mo = jnp.dot(merged.astype(BF16), wo_ref[...], preferred_element_type=F32)
    x1 = x_ref[...] + gate1_ref[...] * mo
    x1_ref[...] = x1
    h2n = _modnorm(x1, g2_ref[...], sc2_ref[...], sh2_ref[...])
    _store_token_tiles(h2_ref, _pack_halves(h2n))
    lt_ref[...] = lax.dot_general(wr_ref[...], h2n, (((1,), (1,)), ((), ())),
                                  precision=lax.Precision.HIGHEST, preferred_element_type=F32)


def _mixer_out(proj, attn_o, x2d, conv_w, wc, wa, wo, mod3, g2, wr_t, seq):
    T, D = x2d.shape
    ts = 256
    per_seq = seq // ts
    halo_rows = 16

    def col(width, off):
        return pl.BlockSpec((ts, width), lambda i: (i, off // width))

    def halo(off):
        return pl.BlockSpec((halo_rows, CONV_WIDTH),
                            lambda i: (jnp.maximum(i * (ts // halo_rows) - 1, 0), off // CONV_WIDTH))

    def const(shape):
        return pl.BlockSpec(shape, lambda i: (0,) * len(shape), pipeline_mode=pl.Buffered(1))

    def mod(k):
        return pl.BlockSpec((None, 1, D), lambda i: ((i // per_seq) * 6 + k, 0, 0))

    return pl.pallas_call(
        functools.partial(_mix_kernel, seq, ts),
        grid=(T // ts,),
        in_specs=[col(CONV_WIDTH, COL_B), col(CONV_WIDTH, COL_C), col(CONV_WIDTH, COL_V),
                  halo(COL_C), halo(COL_V),
                  col(D, COL_GCONV), col(D, COL_GATTN),
                  pl.BlockSpec((ts, ATTN_OUT_WIDTH), lambda i: (i, 0)),
                  pl.BlockSpec((ts, D), lambda i: (i, 0)),
                  const((CONV_K, CONV_WIDTH)), const((CONV_WIDTH, D)), const((ATTN_OUT_WIDTH, D)),
                  const((D, D)),
                  mod(2), const((1, D)), mod(4), mod(3), const((N_EXPERTS, D))],
        out_specs=[pl.BlockSpec((ts, D), lambda i: (i, 0)),
                   pl.BlockSpec((ts * ROW_TILE, LANES), lambda i: (i, 0)),
                   pl.BlockSpec((N_EXPERTS, ts), lambda i: (0, i))],
        out_shape=[jax.ShapeDtypeStruct((T, D), F32),
                   jax.ShapeDtypeStruct((T * ROW_TILE, LANES), U32),
                   jax.ShapeDtypeStruct((N_EXPERTS, T), F32)],
        compiler_params=_params(("arbitrary",), 56),
        name="mixer_out",
    )(proj, proj, proj, proj, proj, proj, proj, attn_o, x2d,
      conv_w, wc, wa, wo, mod3, g2, mod3, mod3, wr_t)


def _route_kernel(lt_ref, bias_ref, eid_ref, rank_ref, w_ref, cnt_ref, carry_ref):
    @pl.when(pl.program_id(0) == 0)
    def _():
        carry_ref[...] = jnp.zeros_like(carry_ref)

    lt = lt_ref[...]
    tr = lt.shape[1]
    per = N_EXPERTS // N_EXPERT_GROUPS
    scores = _sigmoid(lt)
    sel = scores + bias_ref[...]
    sel3 = sel.reshape(N_EXPERT_GROUPS, per, tr)
    sub = lax.broadcasted_iota(jnp.int32, sel3.shape, 1)
    top1 = jnp.max(sel3, axis=1, keepdims=True)
    first = jnp.min(jnp.where(sel3 == top1, sub, per), axis=1, keepdims=True)
    top2 = jnp.max(jnp.where(sub == first, -jnp.inf, sel3), axis=1, keepdims=True)
    gs = (top1 + top2).reshape(N_EXPERT_GROUPS, tr)
    gidx = lax.broadcasted_iota(jnp.int32, gs.shape, 0)
    grank = jnp.zeros(gs.shape, jnp.int32)
    for other in range(N_EXPERT_GROUPS):
        row = gs[other:other + 1, :]
        grank += ((row > gs) | ((row == gs) & (other < gidx))).astype(jnp.int32)
    gmask = grank < TOP_GROUPS
    emask = jnp.broadcast_to(gmask[:, None, :], sel3.shape).reshape(N_EXPERTS, tr)
    cand = jnp.where(emask, sel, NEG_INF)
    eidx = lax.broadcasted_iota(jnp.int32, cand.shape, 0)
    hits, eids, ws = [], [], []
    for _ in range(TOP_K):
        mx = jnp.max(cand, axis=0, keepdims=True)
        idx = jnp.min(jnp.where(cand == mx, eidx, N_EXPERTS), axis=0, keepdims=True)
        hit = eidx == idx
        hits.append(hit)
        eids.append(idx)
        ws.append(jnp.sum(jnp.where(hit, scores, 0.0), axis=0, keepdims=True))
        cand = jnp.where(hit, -jnp.inf, cand)
    chosen = hits[0]
    for hit in hits[1:]:
        chosen = chosen | hit
    chosen = chosen.astype(F32)
    before = (lax.broadcasted_iota(jnp.int32, (tr, tr), 0) < lax.broadcasted_iota(jnp.int32, (tr, tr), 1))
    seen = jnp.dot(chosen.astype(BF16), before.astype(BF16), preferred_element_type=F32) + carry_ref[...]
    ranks = [jnp.sum(jnp.where(hit, seen, 0.0), axis=0, keepdims=True) for hit in hits]
    carry_ref[...] += jnp.sum(chosen, axis=1, keepdims=True)
    cnt_ref[...] = carry_ref[...].astype(jnp.int32)
    wsum = ws[0]
    for w in ws[1:]:
        wsum = wsum + w
    w8 = jnp.concatenate(ws, axis=0) / wsum * ROUTED_SCALE
    w_ref[...] = w8.T
    eid_ref[...] = jnp.concatenate(eids, axis=0)
    rank_ref[...] = jnp.concatenate(ranks, axis=0).astype(jnp.int32)


def _route(logits_t, router_bias):
    E, T = logits_t.shape
    tr = 512
    return pl.pallas_call(
        _route_kernel,
        grid=(T // tr,),
        in_specs=[pl.BlockSpec((E, tr), lambda i: (0, i)),
                  pl.BlockSpec((E, 1), lambda i: (0, 0))],
        out_specs=[pl.BlockSpec((TOP_K, tr), lambda i: (0, i)),
                   pl.BlockSpec((TOP_K, tr), lambda i: (0, i)),
                   pl.BlockSpec((tr, TOP_K), lambda i: (i, 0)),
                   pl.BlockSpec((E, 1), lambda i: (0, 0))],
        out_shape=[jax.ShapeDtypeStruct((TOP_K, T), jnp.int32),
                   jax.ShapeDtypeStruct((TOP_K, T), jnp.int32),
                   jax.ShapeDtypeStruct((T, TOP_K), F32),
                   jax.ShapeDtypeStruct((E, 1), jnp.int32)],
        scratch_shapes=[pltpu.VMEM((E, 1), F32)],
        compiler_params=_params(("arbitrary",), 32),
        name="router",
    )(logits_t, router_bias.reshape(E, 1))


def _slot_kernel(offs_ref, eid_ref, rank_ref, pos_ref):
    eid = eid_ref[...]
    pos = rank_ref[...]
    for e in range(N_EXPERTS):
        pos = pos + jnp.where(eid == e, offs_ref[e], 0)
    pos_ref[...] = pos * ROW_TILE


def _slots(offs, eid, rank):
    return pl.pallas_call(
        _slot_kernel,
        in_specs=[pl.BlockSpec(memory_space=pltpu.SMEM),
                  pl.BlockSpec(memory_space=pltpu.VMEM),
                  pl.BlockSpec(memory_space=pltpu.VMEM)],
        out_specs=pl.BlockSpec(memory_space=pltpu.VMEM),
        out_shape=jax.ShapeDtypeStruct(eid.shape, jnp.int32),
        name="slot_rows",
    )(offs, eid, rank)


def _tile_at(ref, row):
    return ref.at[pl.ds(pl.multiple_of(row, ROW_TILE), ROW_TILE)]


def _dispatch_kernel(tt, pos_ref, h_ref, xs_ref, sem):
    def body(t, _):
        src = _tile_at(h_ref, t * ROW_TILE)
        for k in range(TOP_K):
            pltpu.make_async_copy(src, _tile_at(xs_ref, pos_ref[t * TOP_K + k]), sem).start(priority=k % 2)
        return 0
    lax.fori_loop(0, tt, body, 0)
    for k in range(TOP_K):
        pltpu.make_async_copy(h_ref, xs_ref.at[pl.ds(0, tt * ROW_TILE)], sem).wait()


def _dispatch(pos, h2p):
    T = h2p.shape[0] // ROW_TILE
    tt = 512
    return pl.pallas_call(
        functools.partial(_dispatch_kernel, tt),
        grid=(T // tt,),
        in_specs=[pl.BlockSpec((tt * TOP_K,), lambda i: (i,), memory_space=pltpu.SMEM),
                  pl.BlockSpec((tt * ROW_TILE, LANES), lambda i: (i, 0))],
        out_specs=pl.BlockSpec(memory_space=pl.ANY),
        out_shape=jax.ShapeDtypeStruct((T * TOP_K * ROW_TILE, LANES), U32),
        scratch_shapes=[pltpu.SemaphoreType.DMA],
        compiler_params=_params(("arbitrary",), 32),
        name="dispatch",
    )(pos, h2p)


EXPERT_TILE = 256


def _work_items(counts, n_rows):
    tm = EXPERT_TILE
    n_tiles = n_rows // tm
    ends = jnp.cumsum(counts)
    offs = ends - counts
    cuts = jnp.sort(jnp.concatenate([jnp.arange(n_tiles, dtype=jnp.int32) * tm, offs]))
    nxt = jnp.concatenate([cuts[1:], jnp.array([n_rows], jnp.int32)])
    tile = jnp.minimum(cuts // tm, n_tiles - 1)
    expert = jnp.minimum(jnp.sum((ends[None, :] <= cuts[:, None]).astype(jnp.int32), axis=1), N_EXPERTS - 1)
    lo = cuts - tile * tm
    hi = nxt - tile * tm
    fresh = jnp.concatenate([jnp.ones((1,), jnp.int32), (expert[1:] != expert[:-1]).astype(jnp.int32)])
    n = cuts.shape[0]
    slot = (jnp.cumsum(fresh) - 1) % 2
    fresh_at = jnp.where(fresh == 1, jnp.arange(n, dtype=jnp.int32), n)
    later = jnp.concatenate([lax.cummin(fresh_at, axis=0, reverse=True)[1:], jnp.array([n], jnp.int32)])
    upcoming = jnp.where(later < n, expert[jnp.minimum(later, n - 1)], -1)
    return offs, tile, expert, lo, hi, fresh, slot, upcoming


def _experts_kernel(tile_ref, exp_ref, lo_ref, hi_ref, fresh_ref, slot_ref, upcoming_ref,
                    xs_ref, wg_hbm, wu_hbm, wd_hbm, ys_ref,
                    wg_st, wu_st, wd_st, wgb_ref, wub_ref, wdb_ref, acc_ref, sems):
    i = pl.program_id(0)
    lo = lo_ref[i]
    hi = hi_ref[i]

    def weight_copies(e, s):
        return (pltpu.make_async_copy(wg_hbm.at[e], wg_st.at[s], sems.at[s, 0]),
                pltpu.make_async_copy(wu_hbm.at[e], wu_st.at[s], sems.at[s, 1]),
                pltpu.make_async_copy(wd_hbm.at[e], wd_st.at[s], sems.at[s, 2]))

    @pl.when(fresh_ref[i] == 1)
    def _():
        s = slot_ref[i]

        @pl.when(i == 0)
        def _():
            for cp in weight_copies(exp_ref[i], s):
                cp.start()

        for cp in weight_copies(exp_ref[i], s):
            cp.wait()
        wgb_ref[...] = wg_st[s].astype(BF16)
        wub_ref[...] = wu_st[s].astype(BF16)
        wdb_ref[...] = wd_st[s].astype(BF16)

        @pl.when(upcoming_ref[i] >= 0)
        def _():
            for cp in weight_copies(upcoming_ref[i], 1 - s):
                cp.start()

    def piece():
        x = _unpack_rows_bf16(_load_token_tiles(xs_ref, acc_ref.shape[0]))
        a = jnp.dot(x, wgb_ref[...], preferred_element_type=F32)
        u = jnp.dot(x, wub_ref[...], preferred_element_type=F32)
        row = lax.broadcasted_iota(jnp.int32, (a.shape[0], 1), 0)
        h = jnp.where((row >= lo) & (row < hi), (a * _sigmoid(a)) * u, 0.0)
        return jnp.dot(h.astype(BF16), wdb_ref[...], preferred_element_type=F32)

    @pl.when((hi > lo) & (lo == 0))
    def _():
        acc_ref[...] = piece()

    @pl.when((hi > lo) & (lo > 0))
    def _():
        acc_ref[...] += piece()

    @pl.when((hi <= lo) & (lo == 0))
    def _():
        acc_ref[...] = jnp.zeros_like(acc_ref)

    _store_token_tiles(ys_ref, _pack_halves(acc_ref[...]))


def _experts(items, xs, wg, wu, wd):
    n_rows = xs.shape[0] // ROW_TILE
    _, D, F = wg.shape
    tm = EXPERT_TILE
    n_items = items[0].shape[0]
    tiles = pl.BlockSpec((tm * ROW_TILE, LANES), lambda i, tile, *_: (tile[i], 0))
    hbm = pl.BlockSpec(memory_space=pl.ANY)
    grid_spec = pltpu.PrefetchScalarGridSpec(
        num_scalar_prefetch=len(items),
        grid=(n_items,),
        in_specs=[tiles, hbm, hbm, hbm],
        out_specs=tiles,
        scratch_shapes=[pltpu.VMEM((2, D, F), F32), pltpu.VMEM((2, D, F), F32), pltpu.VMEM((2, F, D), F32),
                        pltpu.VMEM((D, F), BF16), pltpu.VMEM((D, F), BF16), pltpu.VMEM((F, D), BF16),
                        pltpu.VMEM((tm, D), F32), pltpu.SemaphoreType.DMA((2, 3))])
    return pl.pallas_call(
        _experts_kernel,
        grid_spec=grid_spec,
        out_shape=jax.ShapeDtypeStruct((n_rows * ROW_TILE, LANES), U32),
        compiler_params=_params(("arbitrary",), 56),
        name="routed_experts",
    )(*items, xs, wg, wu, wd)


COMBINE_GROUP = 8


def _combine_kernel(tt, n_steps, pos_ref, nxt_ref, w_ref, hp_ref, sg_ref, su_ref, sd_ref, x1_ref, gate2_ref,
                    gf_ref, ys_ref, y_ref, buf_ref, moe_ref, sems):
    i = pl.program_id(0)
    slot = i % 2
    n_groups = tt // COMBINE_GROUP
    half = ROW_TILE * LANES

    def gather_group(p_ref, s, g):
        for tl in range(COMBINE_GROUP):
            t = g * COMBINE_GROUP + tl
            for k in range(TOP_K):
                pltpu.make_async_copy(_tile_at(ys_ref, p_ref[t * TOP_K + k]),
                                      _tile_at(buf_ref.at[s, k], t * ROW_TILE), sems.at[s]).start(priority=k % 2)

    def wait_buffer(s):
        for k in range(TOP_K):
            pltpu.make_async_copy(ys_ref.at[pl.ds(0, tt * ROW_TILE)], buf_ref.at[s, k], sems.at[s]).wait()

    def sum_group(s, g):
        rows = pl.ds(pl.multiple_of(g * COMBINE_GROUP, COMBINE_GROUP), COMBINE_GROUP)
        wg = w_ref[rows, :]
        wk = [wg[:, k:k + 1] for k in range(TOP_K)]
        for c in range(ROW_TILE):
            lo_acc = moe_ref[rows, c * LANES:(c + 1) * LANES]
            hi_acc = moe_ref[rows, half + c * LANES:half + (c + 1) * LANES]
            for k in range(TOP_K):
                word = buf_ref[s, k, pl.ds(g * (COMBINE_GROUP * ROW_TILE) + c, COMBINE_GROUP, stride=ROW_TILE), :]
                lo, hi = _unpack_halves(word)
                lo_acc = lo_acc + wk[k] * lo
                hi_acc = hi_acc + wk[k] * hi
            moe_ref[rows, c * LANES:(c + 1) * LANES] = lo_acc
            moe_ref[rows, half + c * LANES:half + (c + 1) * LANES] = hi_acc

    @pl.when(i == 0)
    def _():
        def first(g, _):
            gather_group(pos_ref, 0, g)
            return 0
        lax.fori_loop(0, n_groups, first, 0)

    x = _unpack_rows_bf16(_load_token_tiles(hp_ref, tt))
    a = jnp.dot(x, sg_ref[...], preferred_element_type=F32)
    u = jnp.dot(x, su_ref[...], preferred_element_type=F32)
    moe_ref[...] = jnp.dot(((a * _sigmoid(a)) * u).astype(BF16), sd_ref[...], preferred_element_type=F32)

    for cur in range(2):
        @pl.when(slot == cur)
        def _(cur=cur):
            wait_buffer(cur)

            def trip(g, _):
                gather_group(nxt_ref, 1 - cur, g)
                sum_group(cur, g)
                return 0
            lax.fori_loop(0, n_groups, trip, 0)

            @pl.when(i == n_steps - 1)
            def _():
                wait_buffer(1 - cur)

    x2 = x1_ref[...] + gate2_ref[...] * moe_ref[...]
    xf = x2 * lax.rsqrt(jnp.mean(x2 * x2, axis=-1, keepdims=True) + EPS)
    y_ref[...] = xf * gf_ref[...]


def _combine(pos, w, h2p, ys, sg, su, sd, x1, mod3, gf, seq):
    T, D = x1.shape
    assert D // 2 == ROW_TILE * LANES, "a packed token row must fill exactly one (8, 128) tile"
    tt = 256
    per_seq = seq // tt

    def const(shape):
        return pl.BlockSpec(shape, lambda i: (0,) * len(shape))

    n_steps = T // tt
    return pl.pallas_call(
        functools.partial(_combine_kernel, tt, n_steps),
        grid=(n_steps,),
        in_specs=[pl.BlockSpec((tt * TOP_K,), lambda i: (i,), memory_space=pltpu.SMEM),
                  pl.BlockSpec((tt * TOP_K,), lambda i: (jnp.minimum(i + 1, n_steps - 1),),
                               memory_space=pltpu.SMEM),
                  pl.BlockSpec((tt, TOP_K), lambda i: (i, 0)),
                  pl.BlockSpec((tt * ROW_TILE, LANES), lambda i: (i, 0)),
                  const(sg.shape), const(su.shape), const(sd.shape),
                  pl.BlockSpec((tt, D), lambda i: (i, 0)),
                  pl.BlockSpec((None, 1, D), lambda i: ((i // per_seq) * 6 + 5, 0, 0)),
                  const((1, D)),
                  pl.BlockSpec(memory_space=pl.ANY)],
        out_specs=pl.BlockSpec((tt, D), lambda i: (i, 0)),
        out_shape=jax.ShapeDtypeStruct((T, D), F32),
        scratch_shapes=[pltpu.VMEM((2, TOP_K, tt * ROW_TILE, LANES), U32), pltpu.VMEM((tt, D), F32),
                        pltpu.SemaphoreType.DMA((2,))],
        compiler_params=_params(("arbitrary",), 56),
        name="combine",
    )(pos, pos, w, h2p, sg, su, sd, x1, mod3, gf, ys)


def kernel(x, c, positions, norm_mix_g, w_ada, b_ada, w_in, conv_w, w_conv_out, w_attn_out,
           w_o, norm_ffn_g, w_router, router_bias, w_exp_gate, w_exp_up, w_exp_down,
           w_sh_gate, w_sh_up, w_sh_down, norm_final_g):
    B, S, D = x.shape
    T = B * S
    assert w_ada.shape[0] == 1, "the final norm is fused into the single layer's combine kernel"
    l = 0
    x2d = x.reshape(T, D)
    mod3 = _ada(c, w_ada[l], b_ada[l]).reshape(B * 6, 1, D)
    h1 = _norm1(x2d, norm_mix_g[l].reshape(1, D), mod3, S)
    proj = _inproj(h1, w_in[l])
    cos2, sins = _rope_tables(positions)
    attn_o = _attention(proj, cos2, sins, B, S)
    x1, h2p, logits_t = _mixer_out(
        proj, attn_o, x2d, conv_w[l], w_conv_out[l].astype(BF16), w_attn_out[l].astype(BF16),
        w_o[l].astype(BF16), mod3, norm_ffn_g[l].reshape(1, D), w_router[l].T, S)
    eid, rank, gate_w, counts = _route(logits_t, router_bias[l])
    offs, *items = _work_items(counts.reshape(N_EXPERTS), T * TOP_K)
    pos = _slots(offs, eid, rank).T.reshape(T * TOP_K)
    xs = _dispatch(pos, h2p)
    ys = _experts(items, xs, w_exp_gate[l], w_exp_up[l], w_exp_down[l])
    y = _combine(pos, gate_w, h2p, ys, w_sh_gate[l].astype(BF16), w_sh_up[l].astype(BF16),
                 w_sh_down[l].astype(BF16), x1, mod3, norm_final_g.reshape(1, D), S)
    return y.reshape(B, S, D)
```

```python
import functools

import jax
import jax.numpy as jnp
from jax import lax
from jax.experimental import pallas as pl
from jax.experimental.pallas import tpu as pltpu

D_MODEL = 2048
HEAD_DIM = 128
ATTN_DILATIONS = (1, 4, 16)
N_GROUPS = 3
HEADS_PER_GROUP = 4
ATTN_WIDTH = N_GROUPS * HEADS_PER_GROUP * HEAD_DIM
ATTN_OUT_WIDTH = HEADS_PER_GROUP * HEAD_DIM
ROPE_THETA = 10000.0
Q_BLOCK = 128
CONV_WIDTH = D_MODEL // 2
CONV_K = 3
IN_COLS = 3 * CONV_WIDTH + 3 * ATTN_WIDTH + 2 * D_MODEL
N_EXPERTS = 64
TOP_K = 8
N_EXPERT_GROUPS = 8
TOP_GROUPS = 4
D_EXPERT = D_MODEL // 4
ROUTED_SCALE = 2.5
EPS = 1e-6
NEG_INF = -1e30

F32 = jnp.float32
BF16 = jnp.bfloat16
U32 = jnp.uint32
MIB = 1024 * 1024

PROJ_BLOCK = 512
N_PROJ_BLOCKS = IN_COLS // PROJ_BLOCK
GATE_SRC_BLOCK = (3 * CONV_WIDTH + 3 * ATTN_WIDTH) // PROJ_BLOCK
N_GATE_BLOCKS = 2 * D_MODEL // PROJ_BLOCK
COL_GCONV = 0
COL_GATTN = D_MODEL
COL_B = 2 * D_MODEL
COL_C = COL_B + CONV_WIDTH
COL_V = COL_C + CONV_WIDTH
COL_Q = COL_V + CONV_WIDTH
COL_K = COL_Q + ATTN_WIDTH
COL_VA = COL_K + ATTN_WIDTH


def _params(semantics, vmem_mib):
    return pltpu.CompilerParams(dimension_semantics=semantics, vmem_limit_bytes=vmem_mib * MIB)


def _sigmoid(x):
    return 1.0 / (1.0 + jnp.exp(-x))


def _pack_halves(x):
    n = x.shape[1] // 2
    return pltpu.pack_elementwise([x[:, :n], x[:, n:]], packed_dtype=BF16)


def _unpack_halves(xp):
    lo = pltpu.unpack_elementwise(xp, index=0, packed_dtype=BF16, unpacked_dtype=F32)
    hi = pltpu.unpack_elementwise(xp, index=1, packed_dtype=BF16, unpacked_dtype=F32)
    return lo, hi


def _unpack_rows_bf16(xp):
    lo, hi = _unpack_halves(xp)
    return jnp.concatenate([lo.astype(BF16), hi.astype(BF16)], axis=1)


ROW_TILE = 8
LANES = 128


def _store_token_tiles(ref, packed):
    n = packed.shape[0]
    for c in range(ROW_TILE):
        ref[pl.ds(c, n, stride=ROW_TILE), :] = packed[:, c * LANES:(c + 1) * LANES]


def _load_token_tiles(ref, n):
    return jnp.concatenate([ref[pl.ds(c, n, stride=ROW_TILE), :] for c in range(ROW_TILE)], axis=1)


def _ada_kernel(c_ref, w_ref, b_ref, o_ref):
    c = c_ref[...]
    s = c * _sigmoid(c)
    o_ref[...] = jnp.dot(s, w_ref[...], precision=lax.Precision.HIGHEST,
                         preferred_element_type=F32) + b_ref[...]


def _ada(c, w_ada, b_ada):
    B, D = c.shape
    N = w_ada.shape[1]
    tn = 2048
    return pl.pallas_call(
        _ada_kernel,
        grid=(N // tn,),
        in_specs=[pl.BlockSpec((B, D), lambda j: (0, 0)),
                  pl.BlockSpec((D, tn), lambda j: (0, j)),
                  pl.BlockSpec((1, tn), lambda j: (0, j))],
        out_specs=pl.BlockSpec((B, tn), lambda j: (0, j)),
        out_shape=jax.ShapeDtypeStruct((B, N), F32),
        compiler_params=_params(("arbitrary",), 48),
        name="ada_mod",
    )(c, w_ada, b_ada.reshape(1, N))


def _modnorm(x, g, scale, shift):
    xf = x * lax.rsqrt(jnp.mean(x * x, axis=-1, keepdims=True) + EPS)
    return (xf * g) * (1.0 + scale) + shift


def _norm1_kernel(x_ref, g_ref, sc_ref, sh_ref, o_ref):
    o_ref[...] = _modnorm(x_ref[...], g_ref[...], sc_ref[...], sh_ref[...]).astype(o_ref.dtype)


def _norm1(x2d, g, mod3, seq):
    T, D = x2d.shape
    tm = 1024
    per_seq = seq // tm
    return pl.pallas_call(
        _norm1_kernel,
        grid=(T // tm,),
        in_specs=[pl.BlockSpec((tm, D), lambda i: (i, 0)),
                  pl.BlockSpec((1, D), lambda i: (0, 0)),
                  pl.BlockSpec((None, 1, D), lambda i: ((i // per_seq) * 6 + 1, 0, 0)),
                  pl.BlockSpec((None, 1, D), lambda i: ((i // per_seq) * 6 + 0, 0, 0))],
        out_specs=pl.BlockSpec((tm, D), lambda i: (i, 0)),
        out_shape=jax.ShapeDtypeStruct((T, D), BF16),
        compiler_params=_params(("arbitrary",), 48),
        name="norm_mix",
    )(x2d, g, mod3, mod3)


def _inproj_kernel(h_ref, w_ref, o_ref, wbf_ref):
    @pl.when(pl.program_id(1) == 0)
    def _():
        wbf_ref[...] = w_ref[...].astype(BF16)

    o_ref[...] = jnp.dot(h_ref[...], wbf_ref[...], preferred_element_type=F32).astype(o_ref.dtype)


def _proj_dst_block(j):
    return jnp.where(j < GATE_SRC_BLOCK, j + N_GATE_BLOCKS, j - GATE_SRC_BLOCK)


def _inproj(h, w_in):
    T, D = h.shape
    tm = 2048
    tn = PROJ_BLOCK
    return pl.pallas_call(
        _inproj_kernel,
        grid=(N_PROJ_BLOCKS, T // tm),
        in_specs=[pl.BlockSpec((tm, D), lambda j, i: (i, 0)),
                  pl.BlockSpec((D, tn), lambda j, i: (0, j))],
        out_specs=pl.BlockSpec((tm, tn), lambda j, i: (i, _proj_dst_block(j))),
        out_shape=jax.ShapeDtypeStruct((T, IN_COLS), BF16),
        scratch_shapes=[pltpu.VMEM((D, tn), BF16)],
        compiler_params=_params(("arbitrary", "arbitrary"), 48),
        name="in_proj",
    )(h, w_in)


def _rope_kernel(pos_ref, inv_ref, sign_ref, cos_ref, sin_ref):
    ang = pos_ref[...].astype(F32) * inv_ref[...]
    cos_ref[...] = jnp.cos(ang)
    sin_ref[...] = jnp.sin(ang) * sign_ref[...]


def _rope_tables(positions):
    T = positions.size
    ts = 1024
    inv = ROPE_THETA ** (-jnp.arange(0, HEAD_DIM, 2, dtype=F32) / HEAD_DIM)
    inv2 = jnp.concatenate([inv, inv]).reshape(1, HEAD_DIM)
    sign = jnp.concatenate([-jnp.ones((HEAD_DIM // 2,), F32), jnp.ones((HEAD_DIM // 2,), F32)]).reshape(1, HEAD_DIM)
    return pl.pallas_call(
        _rope_kernel,
        grid=(T // ts,),
        in_specs=[pl.BlockSpec((ts, 1), lambda i: (i, 0)),
                  pl.BlockSpec((1, HEAD_DIM), lambda i: (0, 0)),
                  pl.BlockSpec((1, HEAD_DIM), lambda i: (0, 0))],
        out_specs=[pl.BlockSpec((ts, HEAD_DIM), lambda i: (i, 0)),
                   pl.BlockSpec((ts, HEAD_DIM), lambda i: (i, 0))],
        out_shape=[jax.ShapeDtypeStruct((T, HEAD_DIM), F32)] * 2,
        compiler_params=_params(("arbitrary",), 32),
        name="rope_tables",
    )(positions.reshape(T, 1), inv2, sign)


ROW_CHUNK = 512


def _attn_group(r, seq, q_ref, k_ref, v_ref, cos_ref, sin_ref, nat_ref, qd_ref, kd_ref, vd_ref,
                acc_ref, m_ref, s_ref):
    n_sub = seq // r
    nb = n_sub // Q_BLOCK
    cls = n_sub + Q_BLOCK
    n_chunks = seq // ROW_CHUNK
    scale = HEAD_DIM ** -0.5

    def rope(t_ref, c0, mult):
        t = t_ref[pl.ds(c0, ROW_CHUNK), :].astype(F32)
        cs = cos_ref[pl.ds(c0, ROW_CHUNK), :]
        sn = sin_ref[pl.ds(c0, ROW_CHUNK), :]
        out = t * cs + pltpu.roll(t, HEAD_DIM // 2, 1) * sn
        return out * mult if mult is not None else out

    def deinterleave(dst_ref, dst_stride, dst_off):
        for rho in range(r):
            dst_ref[pl.ds(rho * dst_stride + dst_off, n_sub), :] = (
                nat_ref[pl.ds(rho, n_sub, stride=r), :].astype(BF16))

    zeros_blk = jnp.zeros((Q_BLOCK, HEAD_DIM), BF16)
    for rho in range(r):
        kd_ref[pl.ds(rho * cls, Q_BLOCK), :] = zeros_blk
        vd_ref[pl.ds(rho * cls, Q_BLOCK), :] = zeros_blk

    if r == 1:
        def fill(c, _):
            c0 = pl.multiple_of(c * ROW_CHUNK, ROW_CHUNK)
            qd_ref[pl.ds(c0, ROW_CHUNK), :] = rope(q_ref, c0, scale).astype(BF16)
            kd_ref[pl.ds(Q_BLOCK + c0, ROW_CHUNK), :] = rope(k_ref, c0, None).astype(BF16)
            vd_ref[pl.ds(Q_BLOCK + c0, ROW_CHUNK), :] = v_ref[pl.ds(c0, ROW_CHUNK), :]
            return 0
        lax.fori_loop(0, n_chunks, fill, 0)
    else:
        def fill_q(c, _):
            c0 = pl.multiple_of(c * ROW_CHUNK, ROW_CHUNK)
            nat_ref[pl.ds(c0, ROW_CHUNK), :] = rope(q_ref, c0, scale)
            return 0
        lax.fori_loop(0, n_chunks, fill_q, 0)
        deinterleave(qd_ref, n_sub, 0)

        def fill_k(c, _):
            c0 = pl.multiple_of(c * ROW_CHUNK, ROW_CHUNK)
            nat_ref[pl.ds(c0, ROW_CHUNK), :] = rope(k_ref, c0, None)
            return 0
        lax.fori_loop(0, n_chunks, fill_k, 0)
        deinterleave(kd_ref, cls, Q_BLOCK)

        def fill_v(c, _):
            c0 = pl.multiple_of(c * ROW_CHUNK, ROW_CHUNK)
            nat_ref[pl.ds(c0, ROW_CHUNK), :] = v_ref[pl.ds(c0, ROW_CHUNK), :].astype(F32)
            return 0
        lax.fori_loop(0, n_chunks, fill_v, 0)
        deinterleave(vd_ref, cls, Q_BLOCK)

    qi = lax.broadcasted_iota(jnp.int32, (Q_BLOCK, 2 * Q_BLOCK), 0)
    kj = lax.broadcasted_iota(jnp.int32, (Q_BLOCK, 2 * Q_BLOCK), 1)
    cur_ok = (kj >= Q_BLOCK) & (kj - Q_BLOCK <= qi)
    prev_ok = (kj < Q_BLOCK) & (kj >= qi)

    def block(blk, _):
        rho = blk // nb
        n = blk % nb
        qrow = pl.multiple_of(rho * n_sub + n * Q_BLOCK, Q_BLOCK)
        krow = pl.multiple_of(rho * cls + n * Q_BLOCK, Q_BLOCK)
        q = qd_ref[pl.ds(qrow, Q_BLOCK), :]
        kw = kd_ref[pl.ds(krow, 2 * Q_BLOCK), :]
        vw = vd_ref[pl.ds(krow, 2 * Q_BLOCK), :]
        s = lax.dot_general(q, kw, (((1,), (1,)), ((), ())), preferred_element_type=F32)
        ok = cur_ok | (prev_ok & (n > 0))
        s = jnp.where(ok, s, NEG_INF)
        m = jnp.max(s, axis=1, keepdims=True)
        p = jnp.exp(s - m)
        ssum = jnp.sum(p, axis=1, keepdims=True)
        acc = jnp.dot(p.astype(BF16), vw, preferred_element_type=F32)
        start = rho + n * (Q_BLOCK * r)
        if r == 1:
            rows = pl.ds(pl.multiple_of(start, Q_BLOCK), Q_BLOCK)
        else:
            rows = pl.ds(start, Q_BLOCK, stride=r)
        acc_ref[rows, :] = acc
        m_ref[rows, :] = jnp.broadcast_to(m, (Q_BLOCK, HEAD_DIM))
        s_ref[rows, :] = jnp.broadcast_to(ssum, (Q_BLOCK, HEAD_DIM))
        return 0

    lax.fori_loop(0, r * nb, block, 0, unroll=8)


def _attn_kernel(seq, q_ref, k_ref, v_ref, cos_ref, sin_ref, o_ref,
                 nat_ref, qd_ref, kd_ref, vd_ref, num_ref, den_ref, max_ref, acc_ref, m_ref, s_ref):
    g = pl.program_id(2)
    n_chunks = seq // ROW_CHUNK
    common = (q_ref, k_ref, v_ref, cos_ref, sin_ref, nat_ref, qd_ref, kd_ref, vd_ref)

    @pl.when(g == 0)
    def _():
        _attn_group(ATTN_DILATIONS[0], seq, *common, num_ref, max_ref, den_ref)

    def merge(last):
        def body(c, _):
            rows = pl.ds(pl.multiple_of(c * ROW_CHUNK, ROW_CHUNK), ROW_CHUNK)
            m_old = max_ref[rows, :]
            m_grp = m_ref[rows, :]
            m_new = jnp.maximum(m_old, m_grp)
            a = jnp.exp(m_old - m_new)
            b = jnp.exp(m_grp - m_new)
            num = num_ref[rows, :] * a + acc_ref[rows, :] * b
            den = den_ref[rows, :] * a + s_ref[rows, :] * b
            if last:
                o_ref[rows, :] = (num / den).astype(o_ref.dtype)
            else:
                num_ref[rows, :] = num
                den_ref[rows, :] = den
                max_ref[rows, :] = m_new
            return 0
        lax.fori_loop(0, n_chunks, body, 0)

    for gi in (1, 2):
        @pl.when(g == gi)
        def _(gi=gi):
            _attn_group(ATTN_DILATIONS[gi], seq, *common, acc_ref, m_ref, s_ref)
            merge(last=(gi == N_GROUPS - 1))


def _attention(proj, cos2, sins, batch, seq):
    T = proj.shape[0]
    qb, kb, vb = COL_Q // HEAD_DIM, COL_K // HEAD_DIM, COL_VA // HEAD_DIM
    pad_rows = seq + max(ATTN_DILATIONS) * Q_BLOCK

    def head_spec(base):
        return pl.BlockSpec((seq, HEAD_DIM), lambda b, j, g: (b, base + g * HEADS_PER_GROUP + j))

    tab_spec = pl.BlockSpec((seq, HEAD_DIM), lambda b, j, g: (b, 0))
    big = pltpu.VMEM((seq, HEAD_DIM), F32)
    return pl.pallas_call(
        functools.partial(_attn_kernel, seq),
        grid=(batch, HEADS_PER_GROUP, N_GROUPS),
        in_specs=[head_spec(qb), head_spec(kb), head_spec(vb), tab_spec, tab_spec],
        out_specs=pl.BlockSpec((seq, HEAD_DIM), lambda b, j, g: (b, j)),
        out_shape=jax.ShapeDtypeStruct((T, ATTN_OUT_WIDTH), BF16),
        scratch_shapes=[big,
                        pltpu.VMEM((seq, HEAD_DIM), BF16),
                        pltpu.VMEM((pad_rows, HEAD_DIM), BF16),
                        pltpu.VMEM((pad_rows, HEAD_DIM), BF16),
                        big, big, big, big, big, big],
        compiler_params=_params(("arbitrary", "arbitrary", "arbitrary"), 48),
        name="dilated_attn",
    )(proj, proj, proj, cos2, sins)


def _mix_kernel(seq, ts,
                b_ref, c_ref, v_ref, ch_ref, vh_ref, gc_ref, ga_ref, ao_ref, x_ref,
                cw_ref, wc_ref, wa_ref, wo_ref, gate1_ref, g2_ref, sc2_ref, sh2_ref, wr_ref,
                x1_ref, h2_ref, lt_ref):
    i = pl.program_id(0)
    not_start = ((i * ts) % seq != 0).astype(F32)
    u = c_ref[...].astype(F32) * v_ref[...].astype(F32)
    halo = ch_ref[...].astype(F32) * vh_ref[...].astype(F32) * not_start
    h1 = halo[15:16, :]
    h2 = halo[14:15, :]
    row = lax.broadcasted_iota(jnp.int32, u.shape, 0)
    up1 = jnp.where(row == 0, h1, pltpu.roll(u, 1, 0))
    up2 = jnp.where(row == 0, h2, jnp.where(row == 1, h1, pltpu.roll(u, 2, 0)))
    cw = cw_ref[...]
    conv = cw[0:1, :] * up2 + cw[1:2, :] * up1 + cw[2:3, :] * u
    yc = jnp.dot((b_ref[...].astype(F32) * conv).astype(BF16), wc_ref[...], preferred_element_type=F32)
    ya = jnp.dot(ao_ref[...], wa_ref[...], preferred_element_type=F32)
    merged = _sigmoid(gc_ref[...].astype(F32)) * yc + _sigmoid(ga_ref[...].astype(F32)) * ya
    mo = jnp.dot(merged.astype(BF16), wo_ref[...], preferred_element_type=F32)
    x1 = x_ref[...] + gate1_ref[...] * mo
    x1_ref[...] = x1
    h2n = _modnorm(x1, g2_ref[...], sc2_ref[...], sh2_ref[...])
    _store_token_tiles(h2_ref, _pack_halves(h2n))
    lt_ref[...] = lax.dot_general(wr_ref[...], h2n, (((1,), (1,)), ((), ())),
                                  precision=lax.Precision.HIGHEST, preferred_element_type=F32)


def _mixer_out(proj, attn_o, x2d, conv_w, wc, wa, wo, mod3, g2, wr_t, seq):
    T, D = x2d.shape
    ts = 256
    per_seq = seq // ts
    halo_rows = 16

    def col(width, off):
        return pl.BlockSpec((ts, width), lambda i: (i, off // width))

    def halo(off):
        return pl.BlockSpec((halo_rows, CONV_WIDTH),
                            lambda i: (jnp.maximum(i * (ts // halo_rows) - 1, 0), off // CONV_WIDTH))

    def const(shape):
        return pl.BlockSpec(shape, lambda i: (0,) * len(shape), pipeline_mode=pl.Buffered(1))

    def mod(k):
        return pl.BlockSpec((None, 1, D), lambda i: ((i // per_seq) * 6 + k, 0, 0))

    return pl.pallas_call(
        functools.partial(_mix_kernel, seq, ts),
        grid=(T // ts,),
        in_specs=[col(CONV_WIDTH, COL_B), col(CONV_WIDTH, COL_C), col(CONV_WIDTH, COL_V),
                  halo(COL_C), halo(COL_V),
                  col(D, COL_GCONV), col(D, COL_GATTN),
                  pl.BlockSpec((ts, ATTN_OUT_WIDTH), lambda i: (i, 0)),
                  pl.BlockSpec((ts, D), lambda i: (i, 0)),
                  const((CONV_K, CONV_WIDTH)), const((CONV_WIDTH, D)), const((ATTN_OUT_WIDTH, D)),
                  const((D, D)),
                  mod(2), const((1, D)), mod(4), mod(3), const((N_EXPERTS, D))],
        out_specs=[pl.BlockSpec((ts, D), lambda i: (i, 0)),
                   pl.BlockSpec((ts * ROW_TILE, LANES), lambda i: (i, 0)),
                   pl.BlockSpec((N_EXPERTS, ts), lambda i: (0, i))],
        out_shape=[jax.ShapeDtypeStruct((T, D), F32),
                   jax.ShapeDtypeStruct((T * ROW_TILE, LANES), U32),
                   jax.ShapeDtypeStruct((N_EXPERTS, T), F32)],
        compiler_params=_params(("arbitrary",), 56),
        name="mixer_out",
    )(proj, proj, proj, proj, proj, proj, proj, attn_o, x2d,
      conv_w, wc, wa, wo, mod3, g2, mod3, mod3, wr_t)


def _route_kernel(lt_ref, bias_ref, eid_ref, rank_ref, w_ref, cnt_ref, carry_ref):
    @pl.when(pl.program_id(0) == 0)
    def _():
        carry_ref[...] = jnp.zeros_like(carry_ref)

    lt = lt_ref[...]
    tr = lt.shape[1]
    per = N_EXPERTS // N_EXPERT_GROUPS
    scores = _sigmoid(lt)
    sel = scores + bias_ref[...]
    sel3 = sel.reshape(N_EXPERT_GROUPS, per, tr)
    sub = lax.broadcasted_iota(jnp.int32, sel3.shape, 1)
    top1 = jnp.max(sel3, axis=1, keepdims=True)
    first = jnp.min(jnp.where(sel3 == top1, sub, per), axis=1, keepdims=True)
    top2 = jnp.max(jnp.where(sub == first, -jnp.inf, sel3), axis=1, keepdims=True)
    gs = (top1 + top2).reshape(N_EXPERT_GROUPS, tr)
    gidx = lax.broadcasted_iota(jnp.int32, gs.shape, 0)
    grank = jnp.zeros(gs.shape, jnp.int32)
    for other in range(N_EXPERT_GROUPS):
        row = gs[other:other + 1, :]
        grank += ((row > gs) | ((row == gs) & (other < gidx))).astype(jnp.int32)
    gmask = grank < TOP_GROUPS
    emask = jnp.broadcast_to(gmask[:, None, :], sel3.shape).reshape(N_EXPERTS, tr)
    cand = jnp.where(emask, sel, NEG_INF)
    eidx = lax.broadcasted_iota(jnp.int32, cand.shape, 0)
    hits, eids, ws = [], [], []
    for _ in range(TOP_K):
        mx = jnp.max(cand, axis=0, keepdims=True)
        idx = jnp.min(jnp.where(cand == mx, eidx, N_EXPERTS), axis=0, keepdims=True)
        hit = eidx == idx
        hits.append(hit)
        eids.append(idx)
        ws.append(jnp.sum(jnp.where(hit, scores, 0.0), axis=0, keepdims=True))
        cand = jnp.where(hit, -jnp.inf, cand)
    chosen = hits[0]
    for hit in hits[1:]:
        chosen = chosen | hit
    chosen = chosen.astype(F32)
    before = (lax.broadcasted_iota(jnp.int32, (tr, tr), 0) < lax.broadcasted_iota(jnp.int32, (tr, tr), 1))
    seen = jnp.dot(chosen.astype(BF16), before.astype(BF16), preferred_element_type=F32) + carry_ref[...]
    ranks = [jnp.sum(jnp.where(hit, seen, 0.0), axis=0, keepdims=True) for hit in hits]
    carry_ref[...] += jnp.sum(chosen, axis=1, keepdims=True)
    cnt_ref[...] = carry_ref[...].astype(jnp.int32)
    wsum = ws[0]
    for w in ws[1:]:
        wsum = wsum + w
    w8 = jnp.concatenate(ws, axis=0) / wsum * ROUTED_SCALE
    w_ref[...] = w8.T
    eid_ref[...] = jnp.concatenate(eids, axis=0)
    rank_ref[...] = jnp.concatenate(ranks, axis=0).astype(jnp.int32)


def _route(logits_t, router_bias):
    E, T = logits_t.shape
    tr = 512
    return pl.pallas_call(
        _route_kernel,
        grid=(T // tr,),
        in_specs=[pl.BlockSpec((E, tr), lambda i: (0, i)),
                  pl.BlockSpec((E, 1), lambda i: (0, 0))],
        out_specs=[pl.BlockSpec((TOP_K, tr), lambda i: (0, i)),
                   pl.BlockSpec((TOP_K, tr), lambda i: (0, i)),
                   pl.BlockSpec((tr, TOP_K), lambda i: (i, 0)),
                   pl.BlockSpec((E, 1), lambda i: (0, 0))],
        out_shape=[jax.ShapeDtypeStruct((TOP_K, T), jnp.int32),
                   jax.ShapeDtypeStruct((TOP_K, T), jnp.int32),
                   jax.ShapeDtypeStruct((T, TOP_K), F32),
                   jax.ShapeDtypeStruct((E, 1), jnp.int32)],
        scratch_shapes=[pltpu.VMEM((E, 1), F32)],
        compiler_params=_params(("arbitrary",), 32),
        name="router",
    )(logits_t, router_bias.reshape(E, 1))


def _slot_kernel(offs_ref, eid_ref, rank_ref, pos_ref):
    eid = eid_ref[...]
    pos = rank_ref[...]
    for e in range(N_EXPERTS):
        pos = pos + jnp.where(eid == e, offs_ref[e], 0)
    pos_ref[...] = pos * ROW_TILE


def _slots(offs, eid, rank):
    return pl.pallas_call(
        _slot_kernel,
        in_specs=[pl.BlockSpec(memory_space=pltpu.SMEM),
                  pl.BlockSpec(memory_space=pltpu.VMEM),
                  pl.BlockSpec(memory_space=pltpu.VMEM)],
        out_specs=pl.BlockSpec(memory_space=pltpu.VMEM),
        out_shape=jax.ShapeDtypeStruct(eid.shape, jnp.int32),
        name="slot_rows",
    )(offs, eid, rank)


def _tile_at(ref, row):
    return ref.at[pl.ds(pl.multiple_of(row, ROW_TILE), ROW_TILE)]


def _dispatch_kernel(tt, pos_ref, h_ref, xs_ref, sem):
    def body(t, _):
        src = _tile_at(h_ref, t * ROW_TILE)
        for k in range(TOP_K):
            pltpu.make_async_copy(src, _tile_at(xs_ref, pos_ref[t * TOP_K + k]), sem).start(priority=k % 2)
        return 0
    lax.fori_loop(0, tt, body, 0)
    for k in range(TOP_K):
        pltpu.make_async_copy(h_ref, xs_ref.at[pl.ds(0, tt * ROW_TILE)], sem).wait()


def _dispatch(pos, h2p):
    T = h2p.shape[0] // ROW_TILE
    tt = 512
    return pl.pallas_call(
        functools.partial(_dispatch_kernel, tt),
        grid=(T // tt,),
        in_specs=[pl.BlockSpec((tt * TOP_K,), lambda i: (i,), memory_space=pltpu.SMEM),
                  pl.BlockSpec((tt * ROW_TILE, LANES), lambda i: (i, 0))],
        out_specs=pl.BlockSpec(memory_space=pl.ANY),
        out_shape=jax.ShapeDtypeStruct((T * TOP_K * ROW_TILE, LANES), U32),
        scratch_shapes=[pltpu.SemaphoreType.DMA],
        compiler_params=_params(("arbitrary",), 32),
        name="dispatch",
    )(pos, h2p)


EXPERT_TILE = 256


def _work_items(counts, n_rows):
    tm = EXPERT_TILE
    n_tiles = n_rows // tm
    ends = jnp.cumsum(counts)
    offs = ends - counts
    cuts = jnp.sort(jnp.concatenate([jnp.arange(n_tiles, dtype=jnp.int32) * tm, offs]))
    nxt = jnp.concatenate([cuts[1:], jnp.array([n_rows], jnp.int32)])
    tile = jnp.minimum(cuts // tm, n_tiles - 1)
    expert = jnp.minimum(jnp.sum((ends[None, :] <= cuts[:, None]).astype(jnp.int32), axis=1), N_EXPERTS - 1)
    lo = cuts - tile * tm
    hi = nxt - tile * tm
    fresh = jnp.concatenate([jnp.ones((1,), jnp.int32), (expert[1:] != expert[:-1]).astype(jnp.int32)])
    n = cuts.shape[0]
    slot = (jnp.cumsum(fresh) - 1) % 2
    fresh_at = jnp.where(fresh == 1, jnp.arange(n, dtype=jnp.int32), n)
    later = jnp.concatenate([lax.cummin(fresh_at, axis=0, reverse=True)[1:], jnp.array([n], jnp.int32)])
    upcoming = jnp.where(later < n, expert[jnp.minimum(later, n - 1)], -1)
    return offs, tile, expert, lo, hi, fresh, slot, upcoming


def _experts_kernel(tile_ref, exp_ref, lo_ref, hi_ref, fresh_ref, slot_ref, upcoming_ref,
                    xs_ref, wg_hbm, wu_hbm, wd_hbm, ys_ref,
                    wg_st, wu_st, wd_st, wgb_ref, wub_ref, wdb_ref, acc_ref, sems):
    i = pl.program_id(0)
    lo = lo_ref[i]
    hi = hi_ref[i]

    def weight_copies(e, s):
        return (pltpu.make_async_copy(wg_hbm.at[e], wg_st.at[s], sems.at[s, 0]),
                pltpu.make_async_copy(wu_hbm.at[e], wu_st.at[s], sems.at[s, 1]),
                pltpu.make_async_copy(wd_hbm.at[e], wd_st.at[s], sems.at[s, 2]))

    @pl.when(fresh_ref[i] == 1)
    def _():
        s = slot_ref[i]

        @pl.when(i == 0)
        def _():
            for cp in weight_copies(exp_ref[i], s):
                cp.start()

        for cp in weight_copies(exp_ref[i], s):
            cp.wait()
        wgb_ref[...] = wg_st[s].astype(BF16)
        wub_ref[...] = wu_st[s].astype(BF16)
        wdb_ref[...] = wd_st[s].astype(BF16)

        @pl.when(upcoming_ref[i] >= 0)
        def _():
            for cp in weight_copies(upcoming_ref[i], 1 - s):
                cp.start()

    def piece():
        x = _unpack_rows_bf16(_load_token_tiles(xs_ref, acc_ref.shape[0]))
        a = jnp.dot(x, wgb_ref[...], preferred_element_type=F32)
        u = jnp.dot(x, wub_ref[...], preferred_element_type=F32)
        row = lax.broadcasted_iota(jnp.int32, (a.shape[0], 1), 0)
        h = jnp.where((row >= lo) & (row < hi), (a * _sigmoid(a)) * u, 0.0)
        return jnp.dot(h.astype(BF16), wdb_ref[...], preferred_element_type=F32)

    @pl.when((hi > lo) & (lo == 0))
    def _():
        acc_ref[...] = piece()

    @pl.when((hi > lo) & (lo > 0))
    def _():
        acc_ref[...] += piece()

    @pl.when((hi <= lo) & (lo == 0))
    def _():
        acc_ref[...] = jnp.zeros_like(acc_ref)

    _store_token_tiles(ys_ref, _pack_halves(acc_ref[...]))


def _experts(items, xs, wg, wu, wd):
    n_rows = xs.shape[0] // ROW_TILE
    _, D, F = wg.shape
    tm = EXPERT_TILE
    n_items = items[0].shape[0]
    tiles = pl.BlockSpec((tm * ROW_TILE, LANES), lambda i, tile, *_: (tile[i], 0))
    hbm = pl.BlockSpec(memory_space=pl.ANY)
    grid_spec = pltpu.PrefetchScalarGridSpec(
        num_scalar_prefetch=len(items),
        grid=(n_items,),
        in_specs=[tiles, hbm, hbm, hbm],
        out_specs=tiles,
        scratch_shapes=[pltpu.VMEM((2, D, F), F32), pltpu.VMEM((2, D, F), F32), pltpu.VMEM((2, F, D), F32),
                        pltpu.VMEM((D, F), BF16), pltpu.VMEM((D, F), BF16), pltpu.VMEM((F, D), BF16),
                        pltpu.VMEM((tm, D), F32), pltpu.SemaphoreType.DMA((2, 3))])
    return pl.pallas_call(
        _experts_kernel,
        grid_spec=grid_spec,
        out_shape=jax.ShapeDtypeStruct((n_rows * ROW_TILE, LANES), U32),
        compiler_params=_params(("arbitrary",), 56),
        name="routed_experts",
    )(*items, xs, wg, wu, wd)


COMBINE_GROUP = 8


def _combine_kernel(tt, n_steps, pos_ref, nxt_ref, w_ref, hp_ref, sg_ref, su_ref, sd_ref, x1_ref, gate2_ref,
                    gf_ref, ys_ref, y_ref, buf_ref, moe_ref, sems):
    i = pl.program_id(0)
    slot = i % 2
    n_groups = tt // COMBINE_GROUP
    half = ROW_TILE * LANES

    def gather_group(p_ref, s, g):
        for tl in range(COMBINE_GROUP):
            t = g * COMBINE_GROUP + tl
            for k in range(TOP_K):
                pltpu.make_async_copy(_tile_at(ys_ref, p_ref[t * TOP_K + k]),
                                      _tile_at(buf_ref.at[s, k], t * ROW_TILE), sems.at[s]).start(priority=k % 2)

    def wait_buffer(s):
        for k in range(TOP_K):
            pltpu.make_async_copy(ys_ref.at[pl.ds(0, tt * ROW_TILE)], buf_ref.at[s, k], sems.at[s]).wait()

    def sum_group(s, g):
        rows = pl.ds(pl.multiple_of(g * COMBINE_GROUP, COMBINE_GROUP), COMBINE_GROUP)
        wg = w_ref[rows, :]
        wk = [wg[:, k:k + 1] for k in range(TOP_K)]
        for c in range(ROW_TILE):
            lo_acc = moe_ref[rows, c * LANES:(c + 1) * LANES]
            hi_acc = moe_ref[rows, half + c * LANES:half + (c + 1) * LANES]
            for k in range(TOP_K):
                word = buf_ref[s, k, pl.ds(g * (COMBINE_GROUP * ROW_TILE) + c, COMBINE_GROUP, stride=ROW_TILE), :]
                lo, hi = _unpack_halves(word)
                lo_acc = lo_acc + wk[k] * lo
                hi_acc = hi_acc + wk[k] * hi
            moe_ref[rows, c * LANES:(c + 1) * LANES] = lo_acc
            moe_ref[rows, half + c * LANES:half + (c + 1) * LANES] = hi_acc

    @pl.when(i == 0)
    def _():
        def first(g, _):
            gather_group(pos_ref, 0, g)
            return 0
        lax.fori_loop(0, n_groups, first, 0)

    x = _unpack_rows_bf16(_load_token_tiles(hp_ref, tt))
    a = jnp.dot(x, sg_ref[...], preferred_element_type=F32)
    u = jnp.dot(x, su_ref[...], preferred_element_type=F32)
    moe_ref[...] = jnp.dot(((a * _sigmoid(a)) * u).astype(BF16), sd_ref[...], preferred_element_type=F32)

    for cur in range(2):
        @pl.when(slot == cur)
        def _(cur=cur):
            wait_buffer(cur)

            def trip(g, _):
                gather_group(nxt_ref, 1 - cur, g)
                sum_group(cur, g)
                return 0
            lax.fori_loop(0, n_groups, trip, 0)

            @pl.when(i == n_steps - 1)
            def _():
                wait_buffer(1 - cur)

    x2 = x1_ref[...] + gate2_ref[...] * moe_ref[...]
    xf = x2 * lax.rsqrt(jnp.mean(x2 * x2, axis=-1, keepdims=True) + EPS)
    y_ref[...] = xf * gf_ref[...]


def _combine(pos, w, h2p, ys, sg, su, sd, x1, mod3, gf, seq):
    T, D = x1.shape
    assert D // 2 == ROW_TILE * LANES, "a packed token row must fill exactly one (8, 128) tile"
    tt = 256
    per_seq = seq // tt

    def const(shape):
        return pl.BlockSpec(shape, lambda i: (0,) * len(shape))

    n_steps = T // tt
    return pl.pallas_call(
        functools.partial(_combine_kernel, tt, n_steps),
        grid=(n_steps,),
        in_specs=[pl.BlockSpec((tt * TOP_K,), lambda i: (i,), memory_space=pltpu.SMEM),
                  pl.BlockSpec((tt * TOP_K,), lambda i: (jnp.minimum(i + 1, n_steps - 1),),
                               memory_space=pltpu.SMEM),
                  pl.BlockSpec((tt, TOP_K), lambda i: (i, 0)),
                  pl.BlockSpec((tt * ROW_TILE, LANES), lambda i: (i, 0)),
                  const(sg.shape), const(su.shape), const(sd.shape),
                  pl.BlockSpec((tt, D), lambda i: (i, 0)),
                  pl.BlockSpec((None, 1, D), lambda i: ((i // per_seq) * 6 + 5, 0, 0)),
                  const((1, D)),
                  pl.BlockSpec(memory_space=pl.ANY)],
        out_specs=pl.BlockSpec((tt, D), lambda i: (i, 0)),
        out_shape=jax.ShapeDtypeStruct((T, D), F32),
        scratch_shapes=[pltpu.VMEM((2, TOP_K, tt * ROW_TILE, LANES), U32), pltpu.VMEM((tt, D), F32),
                        pltpu.SemaphoreType.DMA((2,))],
        compiler_params=_params(("arbitrary",), 56),
        name="combine",
    )(pos, pos, w, h2p, sg, su, sd, x1, mod3, gf, ys)


def kernel(x, c, positions, norm_mix_g, w_ada, b_ada, w_in, conv_w, w_conv_out, w_attn_out,
           w_o, norm_ffn_g, w_router, router_bias, w_exp_gate, w_exp_up, w_exp_down,
           w_sh_gate, w_sh_up, w_sh_down, norm_final_g):
    B, S, D = x.shape
    T = B * S
    assert w_ada.shape[0] == 1, "the final norm is fused into the single layer's combine kernel"
    l = 0
    x2d = x.reshape(T, D)
    mod3 = _ada(c, w_ada[l], b_ada[l]).reshape(B * 6, 1, D)
    h1 = _norm1(x2d, norm_mix_g[l].reshape(1, D), mod3, S)
    proj = _inproj(h1, w_in[l])
    cos2, sins = _rope_tables(positions)
    attn_o = _attention(proj, cos2, sins, B, S)
    x1, h2p, logits_t = _mixer_out(
        proj, attn_o, x2d, conv_w[l], w_conv_out[l].astype(BF16), w_attn_out[l].astype(BF16),
        w_o[l].astype(BF16), mod3, norm_ffn_g[l].reshape(1, D), w_router[l].T, S)
    eid, rank, gate_w, counts = _route(logits_t, router_bias[l])
    offs, *items = _work_items(counts.reshape(N_EXPERTS), T * TOP_K)
    pos = _slots(offs, eid, rank).T.reshape(T * TOP_K)
    xs = _dispatch(pos, h2p)
    ys = _experts(items, xs, w_exp_gate[l], w_exp_up[l], w_exp_down[l])
    y = _combine(pos, gate_w, h2p, ys, w_sh_gate[l].astype(BF16), w_sh_up[l].astype(BF16),
                 w_sh_down[l].astype(BF16), x1, mod3, norm_final_g.reshape(1, D), S)
    return y.reshape(B, S, D)
```

```python
import functools

import jax
import jax.numpy as jnp
from jax import lax
from jax.experimental import pallas as pl
from jax.experimental.pallas import tpu as pltpu

D_MODEL = 2048
HEAD_DIM = 128
ATTN_DILATIONS = (1, 4, 16)
N_GROUPS = 3
HEADS_PER_GROUP = 4
ATTN_WIDTH = N_GROUPS * HEADS_PER_GROUP * HEAD_DIM
ATTN_OUT_WIDTH = HEADS_PER_GROUP * HEAD_DIM
ROPE_THETA = 10000.0
Q_BLOCK = 128
CONV_WIDTH = D_MODEL // 2
CONV_K = 3
IN_COLS = 3 * CONV_WIDTH + 3 * ATTN_WIDTH + 2 * D_MODEL
N_EXPERTS = 64
TOP_K = 8
N_EXPERT_GROUPS = 8
TOP_GROUPS = 4
D_EXPERT = D_MODEL // 4
ROUTED_SCALE = 2.5
EPS = 1e-6
NEG_INF = -1e30

F32 = jnp.float32
BF16 = jnp.bfloat16
U32 = jnp.uint32
MIB = 1024 * 1024

PROJ_BLOCK = 512
N_PROJ_BLOCKS = IN_COLS // PROJ_BLOCK
GATE_SRC_BLOCK = (3 * CONV_WIDTH + 3 * ATTN_WIDTH) // PROJ_BLOCK
N_GATE_BLOCKS = 2 * D_MODEL // PROJ_BLOCK
COL_GCONV = 0
COL_GATTN = D_MODEL
COL_B = 2 * D_MODEL
COL_C = COL_B + CONV_WIDTH
COL_V = COL_C + CONV_WIDTH
COL_Q = COL_V + CONV_WIDTH
COL_K = COL_Q + ATTN_WIDTH
COL_VA = COL_K + ATTN_WIDTH


def _params(semantics, vmem_mib):
    return pltpu.CompilerParams(dimension_semantics=semantics, vmem_limit_bytes=vmem_mib * MIB)


def _sigmoid(x):
    return 1.0 / (1.0 + jnp.exp(-x))


def _pack_halves(x):
    n = x.shape[1] // 2
    return pltpu.pack_elementwise([x[:, :n], x[:, n:]], packed_dtype=BF16)


def _unpack_halves(xp):
    lo = pltpu.unpack_elementwise(xp, index=0, packed_dtype=BF16, unpacked_dtype=F32)
    hi = pltpu.unpack_elementwise(xp, index=1, packed_dtype=BF16, unpacked_dtype=F32)
    return lo, hi


def _unpack_rows_bf16(xp):
    lo, hi = _unpack_halves(xp)
    return jnp.concatenate([lo.astype(BF16), hi.astype(BF16)], axis=1)


ROW_TILE = 8
LANES = 128


def _store_token_tiles(ref, packed):
    n = packed.shape[0]
    for c in range(ROW_TILE):
        ref[pl.ds(c, n, stride=ROW_TILE), :] = packed[:, c * LANES:(c + 1) * LANES]


def _load_token_tiles(ref, n):
    return jnp.concatenate([ref[pl.ds(c, n, stride=ROW_TILE), :] for c in range(ROW_TILE)], axis=1)


def _ada_kernel(c_ref, w_ref, b_ref, o_ref):
    c = c_ref[...]
    s = c * _sigmoid(c)
    o_ref[...] = jnp.dot(s, w_ref[...], precision=lax.Precision.HIGHEST,
                         preferred_element_type=F32) + b_ref[...]


def _ada(c, w_ada, b_ada):
    B, D = c.shape
    N = w_ada.shape[1]
    tn = 1024
    return pl.pallas_call(
        _ada_kernel,
        grid=(N // tn,),
        in_specs=[pl.BlockSpec((B, D), lambda j: (0, 0)),
                  pl.BlockSpec((D, tn), lambda j: (0, j)),
                  pl.BlockSpec((1, tn), lambda j: (0, j))],
        out_specs=pl.BlockSpec((B, tn), lambda j: (0, j)),
        out_shape=jax.ShapeDtypeStruct((B, N), F32),
        compiler_params=_params(("arbitrary",), 40),
        name="ada_mod",
    )(c, w_ada, b_ada.reshape(1, N))


def _modnorm(x, g, scale, shift):
    xf = x * lax.rsqrt(jnp.mean(x * x, axis=-1, keepdims=True) + EPS)
    return (xf * g) * (1.0 + scale) + shift


def _norm1_kernel(x_ref, g_ref, sc_ref, sh_ref, o_ref):
    o_ref[...] = _modnorm(x_ref[...], g_ref[...], sc_ref[...], sh_ref[...]).astype(o_ref.dtype)


def _norm1(x2d, g, mod3, seq):
    T, D = x2d.shape
    tm = 512
    per_seq = seq // tm
    return pl.pallas_call(
        _norm1_kernel,
        grid=(T // tm,),
        in_specs=[pl.BlockSpec((tm, D), lambda i: (i, 0)),
                  pl.BlockSpec((1, D), lambda i: (0, 0)),
                  pl.BlockSpec((None, 1, D), lambda i: ((i // per_seq) * 6 + 1, 0, 0)),
                  pl.BlockSpec((None, 1, D), lambda i: ((i // per_seq) * 6 + 0, 0, 0))],
        out_specs=pl.BlockSpec((tm, D), lambda i: (i, 0)),
        out_shape=jax.ShapeDtypeStruct((T, D), BF16),
        compiler_params=_params(("arbitrary",), 32),
        name="norm_mix",
    )(x2d, g, mod3, mod3)


def _inproj_kernel(h_ref, w_ref, o_ref, wbf_ref):
    @pl.when(pl.program_id(1) == 0)
    def _():
        wbf_ref[...] = w_ref[...].astype(BF16)

    o_ref[...] = jnp.dot(h_ref[...], wbf_ref[...], preferred_element_type=F32).astype(o_ref.dtype)


def _proj_dst_block(j):
    return jnp.where(j < GATE_SRC_BLOCK, j + N_GATE_BLOCKS, j - GATE_SRC_BLOCK)


def _inproj(h, w_in):
    T, D = h.shape
    tm = 2048
    tn = PROJ_BLOCK
    return pl.pallas_call(
        _inproj_kernel,
        grid=(N_PROJ_BLOCKS, T // tm),
        in_specs=[pl.BlockSpec((tm, D), lambda j, i: (i, 0)),
                  pl.BlockSpec((D, tn), lambda j, i: (0, j))],
        out_specs=pl.BlockSpec((tm, tn), lambda j, i: (i, _proj_dst_block(j))),
        out_shape=jax.ShapeDtypeStruct((T, IN_COLS), BF16),
        scratch_shapes=[pltpu.VMEM((D, tn), BF16)],
        compiler_params=_params(("arbitrary", "arbitrary"), 48),
        name="in_proj",
    )(h, w_in)


def _rope_kernel(pos_ref, inv_ref, sign_ref, cos_ref, sin_ref):
    ang = pos_ref[...].astype(F32) * inv_ref[...]
    cos_ref[...] = jnp.cos(ang)
    sin_ref[...] = jnp.sin(ang) * sign_ref[...]


def _rope_tables(positions):
    T = positions.size
    ts = 1024
    inv = ROPE_THETA ** (-jnp.arange(0, HEAD_DIM, 2, dtype=F32) / HEAD_DIM)
    inv2 = jnp.concatenate([inv, inv]).reshape(1, HEAD_DIM)
    sign = jnp.concatenate([-jnp.ones((HEAD_DIM // 2,), F32), jnp.ones((HEAD_DIM // 2,), F32)]).reshape(1, HEAD_DIM)
    return pl.pallas_call(
        _rope_kernel,
        grid=(T // ts,),
        in_specs=[pl.BlockSpec((ts, 1), lambda i: (i, 0)),
                  pl.BlockSpec((1, HEAD_DIM), lambda i: (0, 0)),
                  pl.BlockSpec((1, HEAD_DIM), lambda i: (0, 0))],
        out_specs=[pl.BlockSpec((ts, HEAD_DIM), lambda i: (i, 0)),
                   pl.BlockSpec((ts, HEAD_DIM), lambda i: (i, 0))],
        out_shape=[jax.ShapeDtypeStruct((T, HEAD_DIM), F32)] * 2,
        compiler_params=_params(("arbitrary",), 32),
        name="rope_tables",
    )(positions.reshape(T, 1), inv2, sign)


ROW_CHUNK = 512


def _attn_group(r, seq, q_ref, k_ref, v_ref, cos_ref, sin_ref, nat_ref, qd_ref, kd_ref, vd_ref,
                acc_ref, m_ref, s_ref):
    n_sub = seq // r
    nb = n_sub // Q_BLOCK
    cls = n_sub + Q_BLOCK
    n_chunks = seq // ROW_CHUNK
    scale = HEAD_DIM ** -0.5

    def rope(t_ref, c0, mult):
        t = t_ref[pl.ds(c0, ROW_CHUNK), :].astype(F32)
        cs = cos_ref[pl.ds(c0, ROW_CHUNK), :]
        sn = sin_ref[pl.ds(c0, ROW_CHUNK), :]
        out = t * cs + pltpu.roll(t, HEAD_DIM // 2, 1) * sn
        return out * mult if mult is not None else out

    def deinterleave(dst_ref, dst_stride, dst_off):
        for rho in range(r):
            dst_ref[pl.ds(rho * dst_stride + dst_off, n_sub), :] = (
                nat_ref[pl.ds(rho, n_sub, stride=r), :].astype(BF16))

    zeros_blk = jnp.zeros((Q_BLOCK, HEAD_DIM), BF16)
    for rho in range(r):
        kd_ref[pl.ds(rho * cls, Q_BLOCK), :] = zeros_blk
        vd_ref[pl.ds(rho * cls, Q_BLOCK), :] = zeros_blk

    if r == 1:
        def fill(c, _):
            c0 = pl.multiple_of(c * ROW_CHUNK, ROW_CHUNK)
            qd_ref[pl.ds(c0, ROW_CHUNK), :] = rope(q_ref, c0, scale).astype(BF16)
            kd_ref[pl.ds(Q_BLOCK + c0, ROW_CHUNK), :] = rope(k_ref, c0, None).astype(BF16)
            vd_ref[pl.ds(Q_BLOCK + c0, ROW_CHUNK), :] = v_ref[pl.ds(c0, ROW_CHUNK), :]
            return 0
        lax.fori_loop(0, n_chunks, fill, 0)
    else:
        def fill_q(c, _):
            c0 = pl.multiple_of(c * ROW_CHUNK, ROW_CHUNK)
            nat_ref[pl.ds(c0, ROW_CHUNK), :] = rope(q_ref, c0, scale)
            return 0
        lax.fori_loop(0, n_chunks, fill_q, 0)
        deinterleave(qd_ref, n_sub, 0)

        def fill_k(c, _):
            c0 = pl.multiple_of(c * ROW_CHUNK, ROW_CHUNK)
            nat_ref[pl.ds(c0, ROW_CHUNK), :] = rope(k_ref, c0, None)
            return 0
        lax.fori_loop(0, n_chunks, fill_k, 0)
        deinterleave(kd_ref, cls, Q_BLOCK)

        def fill_v(c, _):
            c0 = pl.multiple_of(c * ROW_CHUNK, ROW_CHUNK)
            nat_ref[pl.ds(c0, ROW_CHUNK), :] = v_ref[pl.ds(c0, ROW_CHUNK), :].astype(F32)
            return 0
        lax.fori_loop(0, n_chunks, fill_v, 0)
        deinterleave(vd_ref, cls, Q_BLOCK)

    qi = lax.broadcasted_iota(jnp.int32, (Q_BLOCK, 2 * Q_BLOCK), 0)
    kj = lax.broadcasted_iota(jnp.int32, (Q_BLOCK, 2 * Q_BLOCK), 1)
    cur_ok = (kj >= Q_BLOCK) & (kj - Q_BLOCK <= qi)
    prev_ok = (kj < Q_BLOCK) & (kj >= qi)

    def block(blk, _):
        rho = blk // nb
        n = blk % nb
        qrow = pl.multiple_of(rho * n_sub + n * Q_BLOCK, Q_BLOCK)
        krow = pl.multiple_of(rho * cls + n * Q_BLOCK, Q_BLOCK)
        q = qd_ref[pl.ds(qrow, Q_BLOCK), :]
        kw = kd_ref[pl.ds(krow, 2 * Q_BLOCK), :]
        vw = vd_ref[pl.ds(krow, 2 * Q_BLOCK), :]
        s = lax.dot_general(q, kw, (((1,), (1,)), ((), ())), preferred_element_type=F32)
        ok = cur_ok | (prev_ok & (n > 0))
        s = jnp.where(ok, s, NEG_INF)
        m = jnp.max(s, axis=1, keepdims=True)
        p = jnp.exp(s - m)
        ssum = jnp.sum(p, axis=1, keepdims=True)
        acc = jnp.dot(p.astype(BF16), vw, preferred_element_type=F32)
        start = rho + n * (Q_BLOCK * r)
        if r == 1:
            rows = pl.ds(pl.multiple_of(start, Q_BLOCK), Q_BLOCK)
        else:
            rows = pl.ds(start, Q_BLOCK, stride=r)
        acc_ref[rows, :] = acc
        m_ref[rows, :] = jnp.broadcast_to(m, (Q_BLOCK, HEAD_DIM))
        s_ref[rows, :] = jnp.broadcast_to(ssum, (Q_BLOCK, HEAD_DIM))
        return 0

    lax.fori_loop(0, r * nb, block, 0, unroll=16)


def _attn_kernel(seq, q_ref, k_ref, v_ref, cos_ref, sin_ref, o_ref,
                 nat_ref, qd_ref, kd_ref, vd_ref, num_ref, den_ref, max_ref, acc_ref, m_ref, s_ref):
    g = pl.program_id(2)
    n_chunks = seq // ROW_CHUNK
    common = (q_ref, k_ref, v_ref, cos_ref, sin_ref, nat_ref, qd_ref, kd_ref, vd_ref)

    @pl.when(g == 0)
    def _():
        _attn_group(ATTN_DILATIONS[0], seq, *common, num_ref, max_ref, den_ref)

    def merge(last):
        def body(c, _):
            rows = pl.ds(pl.multiple_of(c * ROW_CHUNK, ROW_CHUNK), ROW_CHUNK)
            m_old = max_ref[rows, :]
            m_grp = m_ref[rows, :]
            m_new = jnp.maximum(m_old, m_grp)
            a = jnp.exp(m_old - m_new)
            b = jnp.exp(m_grp - m_new)
            num = num_ref[rows, :] * a + acc_ref[rows, :] * b
            den = den_ref[rows, :] * a + s_ref[rows, :] * b
            if last:
                o_ref[rows, :] = (num / den).astype(o_ref.dtype)
            else:
                num_ref[rows, :] = num
                den_ref[rows, :] = den
                max_ref[rows, :] = m_new
            return 0
        lax.fori_loop(0, n_chunks, body, 0)

    for gi in (1, 2):
        @pl.when(g == gi)
        def _(gi=gi):
            _attn_group(ATTN_DILATIONS[gi], seq, *common, acc_ref, m_ref, s_ref)
            merge(last=(gi == N_GROUPS - 1))


def _attention(proj, cos2, sins, batch, seq):
    T = proj.shape[0]
    qb, kb, vb = COL_Q // HEAD_DIM, COL_K // HEAD_DIM, COL_VA // HEAD_DIM
    pad_rows = seq + max(ATTN_DILATIONS) * Q_BLOCK

    def head_spec(base):
        return pl.BlockSpec((seq, HEAD_DIM), lambda b, j, g: (b, base + g * HEADS_PER_GROUP + j))

    tab_spec = pl.BlockSpec((seq, HEAD_DIM), lambda b, j, g: (b, 0))
    big = pltpu.VMEM((seq, HEAD_DIM), F32)
    return pl.pallas_call(
        functools.partial(_attn_kernel, seq),
        grid=(batch, HEADS_PER_GROUP, N_GROUPS),
        in_specs=[head_spec(qb), head_spec(kb), head_spec(vb), tab_spec, tab_spec],
        out_specs=pl.BlockSpec((seq, HEAD_DIM), lambda b, j, g: (b, j)),
        out_shape=jax.ShapeDtypeStruct((T, ATTN_OUT_WIDTH), BF16),
        scratch_shapes=[big,
                        pltpu.VMEM((seq, HEAD_DIM), BF16),
                        pltpu.VMEM((pad_rows, HEAD_DIM), BF16),
                        pltpu.VMEM((pad_rows, HEAD_DIM), BF16),
                        big, big, big, big, big, big],
        compiler_params=_params(("arbitrary", "arbitrary", "arbitrary"), 48),
        name="dilated_attn",
    )(proj, proj, proj, cos2, sins)


def _mix_kernel(seq, ts,
                b_ref, c_ref, v_ref, ch_ref, vh_ref, gc_ref, ga_ref, ao_ref, x_ref,
                cw_ref, wc_ref, wa_ref, wo_ref, gate1_ref, g2_ref, sc2_ref, sh2_ref, wr_ref,
                x1_ref, h2_ref, lt_ref):
    i = pl.program_id(0)
    not_start = ((i * ts) % seq != 0).astype(F32)
    u = c_ref[...].astype(F32) * v_ref[...].astype(F32)
    halo = ch_ref[...].astype(F32) * vh_ref[...].astype(F32) * not_start
    h1 = halo[15:16, :]
    h2 = halo[14:15, :]
    row = lax.broadcasted_iota(jnp.int32, u.shape, 0)
    up1 = jnp.where(row == 0, h1, pltpu.roll(u, 1, 0))
    up2 = jnp.where(row == 0, h2, jnp.where(row == 1, h1, pltpu.roll(u, 2, 0)))
    cw = cw_ref[...]
    conv = cw[0:1, :] * up2 + cw[1:2, :] * up1 + cw[2:3, :] * u
    yc = jnp.dot((b_ref[...].astype(F32) * conv).astype(BF16), wc_ref[...], preferred_element_type=F32)
    ya = jnp.dot(ao_ref[...], wa_ref[...], preferred_element_type=F32)
    merged = _sigmoid(gc_ref[...].astype(F32)) * yc + _sigmoid(ga_ref[...].astype(F32)) * ya
    mo = jnp.dot(merged.astype(BF16), wo_ref[...], preferred_element_type=F32)
    x1 = x_ref[...] + gate1_ref[...] * mo
    x1_ref[...] = x1
    h2n = _modnorm(x1, g2_ref[...], sc2_ref[...], sh2_ref[...])
    _store_token_tiles(h2_ref, _pack_halves(h2n))
    lt_ref[...] = lax.dot_general(wr_ref[...], h2n, (((1,), (1,)), ((), ())),
                                  precision=lax.Precision.HIGHEST, preferred_element_type=F32)


def _mixer_out(proj, attn_o, x2d, conv_w, wc, wa, wo, mod3, g2, wr_t, seq):
    T, D = x2d.shape
    ts = 256
    per_seq = seq // ts
    halo_rows = 16

    def col(width, off):
        return pl.BlockSpec((ts, width), lambda i: (i, off // width))

    def halo(off):
        return pl.BlockSpec((halo_rows, CONV_WIDTH),
                            lambda i: (jnp.maximum(i * (ts // halo_rows) - 1, 0), off // CONV_WIDTH))

    def const(shape):
        return pl.BlockSpec(shape, lambda i: (0,) * len(shape), pipeline_mode=pl.Buffered(1))

    def mod(k):
        return pl.BlockSpec((None, 1, D), lambda i: ((i // per_seq) * 6 + k, 0, 0))

    return pl.pallas_call(
        functools.partial(_mix_kernel, seq, ts),
        grid=(T // ts,),
        in_specs=[col(CONV_WIDTH, COL_B), col(CONV_WIDTH, COL_C), col(CONV_WIDTH, COL_V),
                  halo(COL_C), halo(COL_V),
                  col(D, COL_GCONV), col(D, COL_GATTN),
                  pl.BlockSpec((ts, ATTN_OUT_WIDTH), lambda i: (i, 0)),
                  pl.BlockSpec((ts, D), lambda i: (i, 0)),
                  const((CONV_K, CONV_WIDTH)), const((CONV_WIDTH, D)), const((ATTN_OUT_WIDTH, D)),
                  const((D, D)),
                  mod(2), const((1, D)), mod(4), mod(3), const((N_EXPERTS, D))],
        out_specs=[pl.BlockSpec((ts, D), lambda i: (i, 0)),
                   pl.BlockSpec((ts * ROW_TILE, LANES), lambda i: (i, 0)),
                   pl.BlockSpec((N_EXPERTS, ts), lambda i: (0, i))],
        out_shape=[jax.ShapeDtypeStruct((T, D), F32),
                   jax.ShapeDtypeStruct((T * ROW_TILE, LANES), U32),
                   jax.ShapeDtypeStruct((N_EXPERTS, T), F32)],
        compiler_params=_params(("arbitrary",), 56),
        name="mixer_out",
    )(proj, proj, proj, proj, proj, proj, proj, attn_o, x2d,
      conv_w, wc, wa, wo, mod3, g2, mod3, mod3, wr_t)


def _route_kernel(lt_ref, bias_ref, eid_ref, rank_ref, w_ref, cnt_ref, carry_ref):
    @pl.when(pl.program_id(0) == 0)
    def _():
        carry_ref[...] = jnp.zeros_like(carry_ref)

    lt = lt_ref[...]
    tr = lt.shape[1]
    per = N_EXPERTS // N_EXPERT_GROUPS
    scores = _sigmoid(lt)
    sel = scores + bias_ref[...]
    sel3 = sel.reshape(N_EXPERT_GROUPS, per, tr)
    sub = lax.broadcasted_iota(jnp.int32, sel3.shape, 1)
    top1 = jnp.max(sel3, axis=1, keepdims=True)
    first = jnp.min(jnp.where(sel3 == top1, sub, per), axis=1, keepdims=True)
    top2 = jnp.max(jnp.where(sub == first, -jnp.inf, sel3), axis=1, keepdims=True)
    gs = (top1 + top2).reshape(N_EXPERT_GROUPS, tr)
    gidx = lax.broadcasted_iota(jnp.int32, gs.shape, 0)
    grank = jnp.zeros(gs.shape, jnp.int32)
    for other in range(N_EXPERT_GROUPS):
        row = gs[other:other + 1, :]
        grank += ((row > gs) | ((row == gs) & (other < gidx))).astype(jnp.int32)
    gmask = grank < TOP_GROUPS
    emask = jnp.broadcast_to(gmask[:, None, :], sel3.shape).reshape(N_EXPERTS, tr)
    cand = jnp.where(emask, sel, NEG_INF)
    eidx = lax.broadcasted_iota(jnp.int32, cand.shape, 0)
    hits, eids, ws = [], [], []
    for _ in range(TOP_K):
        mx = jnp.max(cand, axis=0, keepdims=True)
        idx = jnp.min(jnp.where(cand == mx, eidx, N_EXPERTS), axis=0, keepdims=True)
        hit = eidx == idx
        hits.append(hit)
        eids.append(idx)
        ws.append(jnp.sum(jnp.where(hit, scores, 0.0), axis=0, keepdims=True))
        cand = jnp.where(hit, -jnp.inf, cand)
    chosen = hits[0]
    for hit in hits[1:]:
        chosen = chosen | hit
    chosen = chosen.astype(F32)
    before = (lax.broadcasted_iota(jnp.int32, (tr, tr), 0) < lax.broadcasted_iota(jnp.int32, (tr, tr), 1))
    seen = jnp.dot(chosen.astype(BF16), before.astype(BF16), preferred_element_type=F32) + carry_ref[...]
    ranks = [jnp.sum(jnp.where(hit, seen, 0.0), axis=0, keepdims=True) for hit in hits]
    carry_ref[...] += jnp.sum(chosen, axis=1, keepdims=True)
    cnt_ref[...] = carry_ref[...].astype(jnp.int32)
    wsum = ws[0]
    for w in ws[1:]:
        wsum = wsum + w
    w8 = jnp.concatenate(ws, axis=0) / wsum * ROUTED_SCALE
    w_ref[...] = w8.T
    eid_ref[...] = jnp.concatenate(eids, axis=0)
    rank_ref[...] = jnp.concatenate(ranks, axis=0).astype(jnp.int32)


def _route(logits_t, router_bias):
    E, T = logits_t.shape
    tr = 512
    return pl.pallas_call(
        _route_kernel,
        grid=(T // tr,),
        in_specs=[pl.BlockSpec((E, tr), lambda i: (0, i)),
                  pl.BlockSpec((E, 1), lambda i: (0, 0))],
        out_specs=[pl.BlockSpec((TOP_K, tr), lambda i: (0, i)),
                   pl.BlockSpec((TOP_K, tr), lambda i: (0, i)),
                   pl.BlockSpec((tr, TOP_K), lambda i: (i, 0)),
                   pl.BlockSpec((E, 1), lambda i: (0, 0))],
        out_shape=[jax.ShapeDtypeStruct((TOP_K, T), jnp.int32),
                   jax.ShapeDtypeStruct((TOP_K, T), jnp.int32),
                   jax.ShapeDtypeStruct((T, TOP_K), F32),
                   jax.ShapeDtypeStruct((E, 1), jnp.int32)],
        scratch_shapes=[pltpu.VMEM((E, 1), F32)],
        compiler_params=_params(("arbitrary",), 32),
        name="router",
    )(logits_t, router_bias.reshape(E, 1))


def _slot_kernel(offs_ref, eid_ref, rank_ref, pos_ref):
    eid = eid_ref[...]
    pos = rank_ref[...]
    for e in range(N_EXPERTS):
        pos = pos + jnp.where(eid == e, offs_ref[e], 0)
    pos_ref[...] = pos * ROW_TILE


def _slots(offs, eid, rank):
    return pl.pallas_call(
        _slot_kernel,
        in_specs=[pl.BlockSpec(memory_space=pltpu.SMEM),
                  pl.BlockSpec(memory_space=pltpu.VMEM),
                  pl.BlockSpec(memory_space=pltpu.VMEM)],
        out_specs=pl.BlockSpec(memory_space=pltpu.VMEM),
        out_shape=jax.ShapeDtypeStruct(eid.shape, jnp.int32),
        name="slot_rows",
    )(offs, eid, rank)


def _tile_at(ref, row):
    return ref.at[pl.ds(pl.multiple_of(row, ROW_TILE), ROW_TILE)]


def _dispatch_kernel(tt, pos_ref, h_ref, xs_ref, sem):
    def body(t, _):
        src = _tile_at(h_ref, t * ROW_TILE)
        for k in range(TOP_K):
            pltpu.make_async_copy(src, _tile_at(xs_ref, pos_ref[t * TOP_K + k]), sem).start(priority=k % 2)
        return 0
    lax.fori_loop(0, tt, body, 0)
    for k in range(TOP_K):
        pltpu.make_async_copy(h_ref, xs_ref.at[pl.ds(0, tt * ROW_TILE)], sem).wait()


def _dispatch(pos, h2p):
    T = h2p.shape[0] // ROW_TILE
    tt = 512
    return pl.pallas_call(
        functools.partial(_dispatch_kernel, tt),
        grid=(T // tt,),
        in_specs=[pl.BlockSpec((tt * TOP_K,), lambda i: (i,), memory_space=pltpu.SMEM),
                  pl.BlockSpec((tt * ROW_TILE, LANES), lambda i: (i, 0))],
        out_specs=pl.BlockSpec(memory_space=pl.ANY),
        out_shape=jax.ShapeDtypeStruct((T * TOP_K * ROW_TILE, LANES), U32),
        scratch_shapes=[pltpu.SemaphoreType.DMA],
        compiler_params=_params(("arbitrary",), 32),
        name="dispatch",
    )(pos, h2p)


EXPERT_TILE = 256


def _work_items(counts, n_rows):
    tm = EXPERT_TILE
    n_tiles = n_rows // tm
    ends = jnp.cumsum(counts)
    offs = ends - counts
    cuts = jnp.sort(jnp.concatenate([jnp.arange(n_tiles, dtype=jnp.int32) * tm, offs]))
    nxt = jnp.concatenate([cuts[1:], jnp.array([n_rows], jnp.int32)])
    tile = jnp.minimum(cuts // tm, n_tiles - 1)
    expert = jnp.minimum(jnp.sum((ends[None, :] <= cuts[:, None]).astype(jnp.int32), axis=1), N_EXPERTS - 1)
    lo = cuts - tile * tm
    hi = nxt - tile * tm
    fresh = jnp.concatenate([jnp.ones((1,), jnp.int32), (expert[1:] != expert[:-1]).astype(jnp.int32)])
    n = cuts.shape[0]
    slot = (jnp.cumsum(fresh) - 1) % 2
    fresh_at = jnp.where(fresh == 1, jnp.arange(n, dtype=jnp.int32), n)
    later = jnp.concatenate([lax.cummin(fresh_at, axis=0, reverse=True)[1:], jnp.array([n], jnp.int32)])
    upcoming = jnp.where(later < n, expert[jnp.minimum(later, n - 1)], -1)
    return offs, tile, expert, lo, hi, fresh, slot, upcoming


def _experts_kernel(tile_ref, exp_ref, lo_ref, hi_ref, fresh_ref, slot_ref, upcoming_ref,
                    xs_ref, wg_hbm, wu_hbm, wd_hbm, ys_ref,
                    wg_st, wu_st, wd_st, wgb_ref, wub_ref, wdb_ref, acc_ref, sems):
    i = pl.program_id(0)
    lo = lo_ref[i]
    hi = hi_ref[i]

    def weight_copies(e, s):
        return (pltpu.make_async_copy(wg_hbm.at[e], wg_st.at[s], sems.at[s, 0]),
                pltpu.make_async_copy(wu_hbm.at[e], wu_st.at[s], sems.at[s, 1]),
                pltpu.make_async_copy(wd_hbm.at[e], wd_st.at[s], sems.at[s, 2]))

    @pl.when(fresh_ref[i] == 1)
    def _():
        s = slot_ref[i]

        @pl.when(i == 0)
        def _():
            for cp in weight_copies(exp_ref[i], s):
                cp.start()

        for cp in weight_copies(exp_ref[i], s):
            cp.wait()
        wgb_ref[...] = wg_st[s].astype(BF16)
        wub_ref[...] = wu_st[s].astype(BF16)
        wdb_ref[...] = wd_st[s].astype(BF16)

        @pl.when(upcoming_ref[i] >= 0)
        def _():
            for cp in weight_copies(upcoming_ref[i], 1 - s):
                cp.start()

    def piece():
        x = _unpack_rows_bf16(_load_token_tiles(xs_ref, acc_ref.shape[0]))
        a = jnp.dot(x, wgb_ref[...], preferred_element_type=F32)
        u = jnp.dot(x, wub_ref[...], preferred_element_type=F32)
        row = lax.broadcasted_iota(jnp.int32, (a.shape[0], 1), 0)
        h = jnp.where((row >= lo) & (row < hi), (a * _sigmoid(a)) * u, 0.0)
        return jnp.dot(h.astype(BF16), wdb_ref[...], preferred_element_type=F32)

    @pl.when((hi > lo) & (lo == 0))
    def _():
        acc_ref[...] = piece()

    @pl.when((hi > lo) & (lo > 0))
    def _():
        acc_ref[...] += piece()

    @pl.when((hi <= lo) & (lo == 0))
    def _():
        acc_ref[...] = jnp.zeros_like(acc_ref)

    _store_token_tiles(ys_ref, _pack_halves(acc_ref[...]))


def _experts(items, xs, wg, wu, wd):
    n_rows = xs.shape[0] // ROW_TILE
    _, D, F = wg.shape
    tm = EXPERT_TILE
    n_items = items[0].shape[0]
    tiles = pl.BlockSpec((tm * ROW_TILE, LANES), lambda i, tile, *_: (tile[i], 0))
    hbm = pl.BlockSpec(memory_space=pl.ANY)
    grid_spec = pltpu.PrefetchScalarGridSpec(
        num_scalar_prefetch=len(items),
        grid=(n_items,),
        in_specs=[tiles, hbm, hbm, hbm],
        out_specs=tiles,
        scratch_shapes=[pltpu.VMEM((2, D, F), F32), pltpu.VMEM((2, D, F), F32), pltpu.VMEM((2, F, D), F32),
                        pltpu.VMEM((D, F), BF16), pltpu.VMEM((D, F), BF16), pltpu.VMEM((F, D), BF16),
                        pltpu.VMEM((tm, D), F32), pltpu.SemaphoreType.DMA((2, 3))])
    return pl.pallas_call(
        _experts_kernel,
        grid_spec=grid_spec,
        out_shape=jax.ShapeDtypeStruct((n_rows * ROW_TILE, LANES), U32),
        compiler_params=_params(("arbitrary",), 56),
        name="routed_experts",
    )(*items, xs, wg, wu, wd)


COMBINE_GROUP = 8


def _combine_kernel(tt, n_steps, pos_ref, nxt_ref, w_ref, hp_ref, sg_ref, su_ref, sd_ref, x1_ref, gate2_ref,
                    gf_ref, ys_ref, y_ref, buf_ref, moe_ref, sems):
    i = pl.program_id(0)
    slot = i % 2
    n_groups = tt // COMBINE_GROUP
    half = ROW_TILE * LANES

    def gather_group(p_ref, s, g):
        for tl in range(COMBINE_GROUP):
            t = g * COMBINE_GROUP + tl
            for k in range(TOP_K):
                pltpu.make_async_copy(_tile_at(ys_ref, p_ref[t * TOP_K + k]),
                                      _tile_at(buf_ref.at[s, k], t * ROW_TILE), sems.at[s]).start(priority=k % 2)

    def wait_buffer(s):
        for k in range(TOP_K):
            pltpu.make_async_copy(ys_ref.at[pl.ds(0, tt * ROW_TILE)], buf_ref.at[s, k], sems.at[s]).wait()

    def sum_group(s, g):
        rows = pl.ds(pl.multiple_of(g * COMBINE_GROUP, COMBINE_GROUP), COMBINE_GROUP)
        wg = w_ref[rows, :]
        wk = [wg[:, k:k + 1] for k in range(TOP_K)]
        for c in range(ROW_TILE):
            lo_acc = moe_ref[rows, c * LANES:(c + 1) * LANES]
            hi_acc = moe_ref[rows, half + c * LANES:half + (c + 1) * LANES]
            for k in range(TOP_K):
                word = buf_ref[s, k, pl.ds(g * (COMBINE_GROUP * ROW_TILE) + c, COMBINE_GROUP, stride=ROW_TILE), :]
                lo, hi = _unpack_halves(word)
                lo_acc = lo_acc + wk[k] * lo
                hi_acc = hi_acc + wk[k] * hi
            moe_ref[rows, c * LANES:(c + 1) * LANES] = lo_acc
            moe_ref[rows, half + c * LANES:half + (c + 1) * LANES] = hi_acc

    @pl.when(i == 0)
    def _():
        def first(g, _):
            gather_group(pos_ref, 0, g)
            return 0
        lax.fori_loop(0, n_groups, first, 0)

    x = _unpack_rows_bf16(_load_token_tiles(hp_ref, tt))
    a = jnp.dot(x, sg_ref[...], preferred_element_type=F32)
    u = jnp.dot(x, su_ref[...], preferred_element_type=F32)
    moe_ref[...] = jnp.dot(((a * _sigmoid(a)) * u).astype(BF16), sd_ref[...], preferred_element_type=F32)

    for cur in range(2):
        @pl.when(slot == cur)
        def _(cur=cur):
            wait_buffer(cur)

            def trip(g, _):
                gather_group(nxt_ref, 1 - cur, g)
                sum_group(cur, g)
                return 0
            lax.fori_loop(0, n_groups, trip, 0)

            @pl.when(i == n_steps - 1)
            def _():
                wait_buffer(1 - cur)

    x2 = x1_ref[...] + gate2_ref[...] * moe_ref[...]
    xf = x2 * lax.rsqrt(jnp.mean(x2 * x2, axis=-1, keepdims=True) + EPS)
    y_ref[...] = xf * gf_ref[...]


def _combine(pos, w, h2p, ys, sg, su, sd, x1, mod3, gf, seq):
    T, D = x1.shape
    assert D // 2 == ROW_TILE * LANES, "a packed token row must fill exactly one (8, 128) tile"
    tt = 256
    per_seq = seq // tt

    def const(shape):
        return pl.BlockSpec(shape, lambda i: (0,) * len(shape))

    n_steps = T // tt
    return pl.pallas_call(
        functools.partial(_combine_kernel, tt, n_steps),
        grid=(n_steps,),
        in_specs=[pl.BlockSpec((tt * TOP_K,), lambda i: (i,), memory_space=pltpu.SMEM),
                  pl.BlockSpec((tt * TOP_K,), lambda i: (jnp.minimum(i + 1, n_steps - 1),),
                               memory_space=pltpu.SMEM),
                  pl.BlockSpec((tt, TOP_K), lambda i: (i, 0)),
                  pl.BlockSpec((tt * ROW_TILE, LANES), lambda i: (i, 0)),
                  const(sg.shape), const(su.shape), const(sd.shape),
                  pl.BlockSpec((tt, D), lambda i: (i, 0)),
                  pl.BlockSpec((None, 1, D), lambda i: ((i // per_seq) * 6 + 5, 0, 0)),
                  const((1, D)),
                  pl.BlockSpec(memory_space=pl.ANY)],
        out_specs=pl.BlockSpec((tt, D), lambda i: (i, 0)),
        out_shape=jax.ShapeDtypeStruct((T, D), F32),
        scratch_shapes=[pltpu.VMEM((2, TOP_K, tt * ROW_TILE, LANES), U32), pltpu.VMEM((tt, D), F32),
                        pltpu.SemaphoreType.DMA((2,))],
        compiler_params=_params(("arbitrary",), 56),
        name="combine",
    )(pos, pos, w, h2p, sg, su, sd, x1, mod3, gf, ys)


def kernel(x, c, positions, norm_mix_g, w_ada, b_ada, w_in, conv_w, w_conv_out, w_attn_out,
           w_o, norm_ffn_g, w_router, router_bias, w_exp_gate, w_exp_up, w_exp_down,
           w_sh_gate, w_sh_up, w_sh_down, norm_final_g):
    B, S, D = x.shape
    T = B * S
    assert w_ada.shape[0] == 1, "the final norm is fused into the single layer's combine kernel"
    l = 0
    x2d = x.reshape(T, D)
    mod3 = _ada(c, w_ada[l], b_ada[l]).reshape(B * 6, 1, D)
    h1 = _norm1(x2d, norm_mix_g[l].reshape(1, D), mod3, S)
    proj = _inproj(h1, w_in[l])
    cos2, sins = _rope_tables(positions)
    attn_o = _attention(proj, cos2, sins, B, S)
    x1, h2p, logits_t = _mixer_out(
        proj, attn_o, x2d, conv_w[l], w_conv_out[l].astype(BF16), w_attn_out[l].astype(BF16),
        w_o[l].astype(BF16), mod3, norm_ffn_g[l].reshape(1, D), w_router[l].T, S)
    eid, rank, gate_w, counts = _route(logits_t, router_bias[l])
    offs, *items = _work_items(counts.reshape(N_EXPERTS), T * TOP_K)
    pos = _slots(offs, eid, rank).T.reshape(T * TOP_K)
    xs = _dispatch(pos, h2p)
    ys = _experts(items, xs, w_exp_gate[l], w_exp_up[l], w_exp_down[l])
    y = _combine(pos, gate_w, h2p, ys, w_sh_gate[l].astype(BF16), w_sh_up[l].astype(BF16),
                 w_sh_down[l].astype(BF16), x1, mod3, norm_final_g.reshape(1, D), S)
    return y.reshape(B, S, D)
```
